```python
import math
import jax, jax.numpy as jnp
from jax import lax
import numpy as np

D_MODEL = 1024
BATCH = 2
SEQ = 8192
DEPTH = 4

N_A_LAYERS = DEPTH // 2
N_B_LAYERS = DEPTH - N_A_LAYERS
EPS = 1e-6

A_HEADS = 8
A_DK = 128
A_DV = 256
A_QK_W = A_HEADS * A_DK
A_V_W = A_HEADS * A_DV
A_CONV_CH = 2 * A_QK_W + A_V_W
A_IN_W = A_CONV_CH + A_V_W + 2 * A_HEADS
CONV_W = 4
CHUNK = 64

B_Q_HEADS = 32
B_KV_HEADS = 4
B_GROUP = B_Q_HEADS // B_KV_HEADS
B_HD = 64
B_W = B_Q_HEADS * B_HD
B_KV_W = B_KV_HEADS * B_HD
WINDOW = 128
BLOCK = 128

N_BUCKETS = 32
MAX_DIST = 128

kernel_name = "yoco_gated_deltanet_swa_sink_hybrid"


def rms_norm(x, g):
    xf = x.astype(jnp.float32)
    y = xf * lax.rsqrt(jnp.mean(xf * xf, axis=-1, keepdims=True) + EPS)
    return (y * g.astype(jnp.float32)).astype(x.dtype)


def l2_norm(x):
    xf = x.astype(jnp.float32)
    return xf * lax.rsqrt(jnp.sum(xf * xf, axis=-1, keepdims=True) + EPS)


def causal_depthwise_conv(x, w):
    c = x.shape[-1]
    return lax.conv_general_dilated(
        x, w[:, None, :].astype(x.dtype), window_strides=(1,),
        padding=[(CONV_W - 1, 0)], dimension_numbers=("NWC", "WIO", "NWC"),
        feature_group_count=c)


def chunk_gated_delta_rule(q, k, v, g, beta):
    B, T, H, _ = q.shape
    N = T // CHUNK

    def chunks(t):
        return t.reshape(B, N, CHUNK, H, -1).transpose(0, 1, 3, 2, 4)

    q, k, v = chunks(q), chunks(k), chunks(v)
    g = jnp.cumsum(g.reshape(B, N, CHUNK, H).transpose(0, 1, 3, 2), axis=-1)
    beta = beta.reshape(B, N, CHUNK, H).transpose(0, 1, 3, 2)

    tri = jnp.tril(jnp.ones((CHUNK, CHUNK), dtype=bool))
    strict = jnp.tril(jnp.ones((CHUNK, CHUNK), dtype=bool), -1)
    decay = jnp.exp(jnp.where(tri, g[..., :, None] - g[..., None, :], -jnp.inf))

    kk = jnp.einsum("bnhcd,bnhsd->bnhcs", k, k)
    lower = jnp.where(strict, beta[..., :, None] * kk * decay, 0.0)
    rhs = jnp.concatenate([v * beta[..., None], k * (beta * jnp.exp(g))[..., None]], axis=-1)
    sol = lax.linalg.triangular_solve(lower, rhs, left_side=True, lower=True, unit_diagonal=True)
    u, w = sol[..., :A_DV], sol[..., A_DV:]

    qk = jnp.einsum("bnhcd,bnhsd->bnhcs", q, k) * decay
    q_dec = q * jnp.exp(g)[..., None]
    k_dec = k * jnp.exp(g[..., -1:] - g)[..., None]
    g_last = jnp.exp(g[..., -1])

    def step(S, inp):
        qk_i, qd_i, kd_i, u_i, w_i, gl_i = inp
        v_new = u_i - jnp.einsum("bhck,bhkv->bhcv", w_i, S)
        o = jnp.einsum("bhck,bhkv->bhcv", qd_i, S) + jnp.einsum("bhcs,bhsv->bhcv", qk_i, v_new)
        S = S * gl_i[..., None, None] + jnp.einsum("bhck,bhcv->bhkv", kd_i, v_new)
        return S, o

    xs = tuple(t.swapaxes(0, 1) for t in (qk, q_dec, k_dec, u, w, g_last))
    S0 = jnp.zeros((B, H, A_DK, A_DV), jnp.float32)
    _, o = lax.scan(step, S0, xs)
    return o.transpose(1, 0, 3, 2, 4).reshape(B, T, H, A_DV)


def gated_deltanet_mixer(h, w_in, conv_w, a_log, dt_bias, o_gain, w_out):
    B, T, _ = h.shape
    proj = h @ w_in
    qkv, z, ab = jnp.split(proj, [A_CONV_CH, A_CONV_CH + A_V_W], axis=-1)
    qkv = jax.nn.silu(causal_depthwise_conv(qkv, conv_w))
    q, k, v = jnp.split(qkv, [A_QK_W, 2 * A_QK_W], axis=-1)
    q = l2_norm(q.reshape(B, T, A_HEADS, A_DK)) * (A_DK ** -0.5)
    k = l2_norm(k.reshape(B, T, A_HEADS, A_DK))
    v = v.reshape(B, T, A_HEADS, A_DV).astype(jnp.float32)
    b_logit, a_logit = jnp.split(ab.astype(jnp.float32), 2, axis=-1)
    beta = jax.nn.sigmoid(b_logit)
    g = -jnp.exp(a_log.astype(jnp.float32)) * jax.nn.softplus(a_logit + dt_bias.astype(jnp.float32))
    o = chunk_gated_delta_rule(q, k, v, g, beta)
    o = rms_norm(o, o_gain) * jax.nn.silu(z.reshape(B, T, A_HEADS, A_DV).astype(jnp.float32))
    return o.reshape(B, T, A_V_W).astype(h.dtype) @ w_out


def shared_kv(x, kv_norm, w_kv, k_gain):
    B, T, _ = x.shape
    nb = T // BLOCK
    kv = rms_norm(x, kv_norm) @ w_kv
    k, v = jnp.split(kv, 2, axis=-1)
    k = rms_norm(k.reshape(B, T, B_KV_HEADS, B_HD), k_gain)
    v = v.reshape(B, T, B_KV_HEADS, B_HD)

    def band(t):
        tb = t.reshape(B, nb, BLOCK, B_KV_HEADS, B_HD)
        prev = jnp.concatenate([jnp.zeros_like(tb[:, :1]), tb[:, :-1]], axis=1)
        return jnp.concatenate([prev, tb], axis=2)

    return band(k), band(v)


def t5_causal_bucket(dist):
    n = np.maximum(dist, 0)
    max_exact = N_BUCKETS // 2
    large = max_exact + (np.log(np.maximum(n, 1) / max_exact) / np.log(MAX_DIST / max_exact)
                         * (N_BUCKETS - max_exact)).astype(np.int64)
    large = np.minimum(large, N_BUCKETS - 1)
    return np.where(n < max_exact, n, large).astype(np.int32)


def band_bias_and_mask(rel_bias, seq):
    nb = seq // BLOCK
    qi = np.arange(BLOCK)[:, None]
    s = np.arange(2 * BLOCK)[None, :]
    dist = qi + BLOCK - s
    in_window = (dist >= 0) & (dist < WINDOW)
    bias = rel_bias.astype(jnp.float32)[jnp.asarray(t5_causal_bucket(dist))]
    bias = bias.transpose(2, 0, 1).reshape(B_KV_HEADS, B_GROUP, BLOCK, 2 * BLOCK)
    key_pos = np.arange(nb)[:, None] * BLOCK - BLOCK + s
    mask = in_window[None] & (key_pos >= 0)[:, None, :]
    return bias, jnp.asarray(mask)


def swa_sink_mixer(h, w_in, q_gain, sinks, w_out, k_band, v_band, bias, mask):
    B, T, _ = h.shape
    nb = T // BLOCK
    q, z = jnp.split(h @ w_in, 2, axis=-1)
    q = rms_norm(q.reshape(B, T, B_Q_HEADS, B_HD), q_gain) * (B_HD ** -0.5)
    q = q.reshape(B, nb, BLOCK, B_KV_HEADS, B_GROUP, B_HD)
    logits = jnp.einsum("bnqkgd,bnskd->bnkgqs", q, k_band).astype(jnp.float32) + bias
    logits = jnp.where(mask[None, :, None, None], logits, -jnp.inf)
    sink = sinks.astype(jnp.float32).reshape(B_KV_HEADS, B_GROUP)[..., None, None]
    m = jnp.maximum(jnp.max(logits, axis=-1, keepdims=True), sink)
    p = jnp.exp(logits - m)
    probs = p / (jnp.sum(p, axis=-1, keepdims=True) + jnp.exp(sink - m))
    o = jnp.einsum("bnkgqs,bnskd->bnqkgd", probs.astype(v_band.dtype), v_band)
    o = o.reshape(B, T, B_W) * jax.nn.silu(z)
    return o @ w_out


def setup_inputs(seed: int = 0) -> dict:
    key = jax.random.key(seed)
    ks = jax.random.split(key, 20)
    f = jnp.float32
    D = D_MODEL
    nA, nB = N_A_LAYERS, N_B_LAYERS

    def nrm(k, shape, scale):
        return jax.random.normal(k, shape, f) * scale

    dt = jnp.exp(jax.random.uniform(ks[5], (nA, A_HEADS), f, math.log(1e-3), math.log(1e-1)))
    return {
        "x": nrm(ks[0], (BATCH, SEQ, D), 1.0),
        "a_norm": 1.0 + nrm(ks[1], (nA, D), 0.02),
        "a_w_in": nrm(ks[2], (nA, D, A_IN_W), D ** -0.5),
        "a_conv": nrm(ks[3], (nA, CONV_W, A_CONV_CH), CONV_W ** -0.5),
        "a_A_log": jnp.log(jax.random.uniform(ks[4], (nA, A_HEADS), f, 1.0, 16.0)),
        "a_dt_bias": dt + jnp.log(-jnp.expm1(-dt)),
        "a_o_gain": 1.0 + nrm(ks[6], (nA, A_DV), 0.02),
        "a_w_out": nrm(ks[7], (nA, A_V_W, D), A_V_W ** -0.5),
        "kv_norm": 1.0 + nrm(ks[8], (D,), 0.02),
        "w_kv": nrm(ks[9], (D, 2 * B_KV_W), D ** -0.5),
        "k_gain": 1.0 + nrm(ks[10], (B_HD,), 0.02),
        "rel_bias": nrm(ks[11], (N_BUCKETS, B_Q_HEADS), 0.5),
        "b_norm": 1.0 + nrm(ks[12], (nB, D), 0.02),
        "b_w_in": nrm(ks[13], (nB, D, 2 * B_W), D ** -0.5),
        "b_q_gain": 1.0 + nrm(ks[14], (nB, B_HD), 0.02),
        "b_sinks": nrm(ks[15], (nB, B_Q_HEADS), 0.5),
        "b_w_out": nrm(ks[16], (nB, B_W, D), B_W ** -0.5),
    }


def reference(x, a_norm, a_w_in, a_conv, a_A_log, a_dt_bias, a_o_gain, a_w_out,
              kv_norm, w_kv, k_gain, rel_bias,
              b_norm, b_w_in, b_q_gain, b_sinks, b_w_out):
    bias, mask = band_bias_and_mask(rel_bias, x.shape[1])
    k_band = v_band = None
    for layer in range(DEPTH):
        if layer < N_A_LAYERS:
            i = layer
            x = x + gated_deltanet_mixer(rms_norm(x, a_norm[i]), a_w_in[i], a_conv[i], a_A_log[i],
                                         a_dt_bias[i], a_o_gain[i], a_w_out[i])
            if layer == N_A_LAYERS - 1:
                k_band, v_band = shared_kv(x, kv_norm, w_kv, k_gain)
        else:
            j = layer - N_A_LAYERS
            x = x + swa_sink_mixer(rms_norm(x, b_norm[j]), b_w_in[j], b_q_gain[j], b_sinks[j],
                                   b_w_out[j], k_band, v_band, bias, mask)
    return x
```

```python
import functools
import math

import numpy as np
import jax
import jax.numpy as jnp
from jax import lax
from jax.experimental import pallas as pl
from jax.experimental.pallas import tpu as pltpu

F32 = jnp.float32
BF16 = jnp.bfloat16

D_MODEL = 1024
EPS = 1e-6

A_HEADS = 8
A_DK = 128
A_DV = 256
A_QK_W = A_HEADS * A_DK
A_V_W = A_HEADS * A_DV
A_CONV_CH = 2 * A_QK_W + A_V_W
CONV_W = 4
CHUNK = 64
A_TILE = 256
CONV_PAD = 8

B_Q_HEADS = 32
B_KV_HEADS = 4
B_GROUP = B_Q_HEADS // B_KV_HEADS
B_HD = 64
B_W = B_Q_HEADS * B_HD
B_KV_W = B_KV_HEADS * B_HD
WINDOW = 128
BLOCK = 128
B_TILE = 256
KV_TILE = 512
N_BUCKETS = 32
MAX_DIST = 128
NEG = -1e30

LANES = 128
VMEM_LIMIT = 56 * 1024 * 1024

_NT = (((1,), (1,)), ((), ()))
_NN = (((1,), (0,)), ((), ()))
_TN = (((0,), (0,)), ((), ()))


def _mm(a, b, dims=_NN):
    return lax.dot_general(a.astype(BF16), b.astype(BF16), dims, preferred_element_type=F32)


def _mm32(a, b, dims=_NN):
    return lax.dot_general(a, b, dims, precision=lax.Precision.HIGHEST, preferred_element_type=F32)


def _silu(x):
    return x * jax.nn.sigmoid(x)


def _rms_rows(x, gain):
    return x * lax.rsqrt(jnp.mean(x * x, axis=-1, keepdims=True) + EPS) * gain


def _const_spec(shape):
    nd = len(shape)
    return pl.BlockSpec(shape, lambda *_: (0,) * nd, pipeline_mode=pl.Buffered(1))


def _deltanet_kernel(x_ref, norm_ref, wqkv_ref, wz_ref, wab_ref, wabt_ref, conv_ref,
                     alog_ref, dtb_ref, alog_c_ref, dtb_c_ref, ogain_ref, wout_ref,
                     out_ref,
                     ext_ref, state_ref, q_ref, k_ref, v_ref, og_ref, vnew_ref, ointer_ref):
    tm = A_TILE
    n_chunks = tm // CHUNK
    log_c = int(math.log2(CHUNK))
    first = pl.program_id(1) == 0

    @pl.when(first)
    def _():
        ext_ref[0:CONV_PAD, :] = jnp.zeros((CONV_PAD, A_CONV_CH), F32)
        state_ref[...] = jnp.zeros_like(state_ref)

    x = x_ref[0]
    hb = _rms_rows(x, norm_ref[...]).astype(BF16)

    slab = 1024
    for s in range(A_CONV_CH // slab):
        ext_ref[CONV_PAD:CONV_PAD + tm, s * slab:(s + 1) * slab] = _mm(hb, wqkv_ref[:, s * slab:(s + 1) * slab])

    for s in range(A_CONV_CH // LANES):
        cols = slice(s * LANES, (s + 1) * LANES)
        acc = ext_ref[CONV_PAD:CONV_PAD + tm, cols] * conv_ref[CONV_W - 1:CONV_W, cols]
        for j in range(CONV_W - 1):
            off = CONV_PAD - (CONV_W - 1) + j
            acc = acc + ext_ref[off:off + tm, cols] * conv_ref[j:j + 1, cols]
        y = _silu(acc)
        if s < A_HEADS:
            y = y * (lax.rsqrt(jnp.sum(y * y, axis=-1, keepdims=True) + EPS) * (A_DK ** -0.5))
            q_ref[:, cols] = y
        elif s < 2 * A_HEADS:
            y = y * lax.rsqrt(jnp.sum(y * y, axis=-1, keepdims=True) + EPS)
            k_ref[:, s * LANES - A_QK_W:(s + 1) * LANES - A_QK_W] = y
        else:
            v_ref[:, s * LANES - 2 * A_QK_W:(s + 1) * LANES - 2 * A_QK_W] = y

    ext_ref[0:CONV_PAD, :] = ext_ref[tm:tm + CONV_PAD, :]

    ab = _mm(hb, wab_ref[...])
    ab_t = lax.dot_general(wabt_ref[...], hb, _NT, preferred_element_type=F32)
    beta_c = jax.nn.sigmoid(ab[:, 0:A_HEADS])
    g_c = -jnp.exp(alog_ref[...]) * jax.nn.softplus(ab[:, A_HEADS:2 * A_HEADS] + dtb_ref[...])
    g_r = -jnp.exp(alog_c_ref[...]) * jax.nn.softplus(ab_t[A_HEADS:2 * A_HEADS, :] + dtb_c_ref[...])

    ri = lax.broadcasted_iota(jnp.int32, (tm, tm), 0)
    ci = lax.broadcasted_iota(jnp.int32, (tm, tm), 1)
    same = (ri >> log_c) == (ci >> log_c)
    incl = same & (ri >= ci)
    strict = same & (ri > ci)
    eye = (ri == ci).astype(F32)
    gcum_c = _mm32(incl.astype(F32), g_c)
    gtot_c = _mm32(same.astype(F32), g_c)
    gcum_r = _mm32(g_r, (same & (ri <= ci)).astype(F32))

    for h in range(A_HEADS):
        qh = q_ref[:, h * A_DK:(h + 1) * A_DK]
        kh = k_ref[:, h * A_DK:(h + 1) * A_DK]
        vh = v_ref[:, h * A_DV:(h + 1) * A_DV]
        gc = gcum_c[:, h:h + 1]
        gt = gtot_c[:, h:h + 1]
        gr = gcum_r[h:h + 1, :]
        bc = beta_c[:, h:h + 1]
        eg = jnp.exp(gc)

        decay = jnp.where(incl, jnp.exp(jnp.minimum(gc - gr, 0.0)), 0.0)
        qk_kk = _mm(jnp.concatenate([qh, kh], axis=0), kh, _NT)
        a_qk = qk_kk[0:tm] * decay
        low = jnp.where(strict, qk_kk[tm:2 * tm] * decay * bc, 0.0)

        t_inv = eye - low
        pw = low
        for _ in range(log_c - 1):
            pw = _mm(pw, pw)
            t_inv = t_inv + _mm(t_inv, pw)

        u = _mm(t_inv, vh * bc)
        w = _mm(t_inv, kh * (bc * eg))
        q_dec = qh * eg
        k_dec = kh * jnp.exp(gt - gc)

        for j in range(n_chunks):
            r = slice(j * CHUNK, (j + 1) * CHUNK)
            st = state_ref[h]
            wq = _mm(jnp.concatenate([w[r], q_dec[r]], axis=0), st)
            v_new = u[r] - wq[0:CHUNK]
            vnew_ref[r, :] = v_new
            ointer_ref[r, :] = wq[CHUNK:2 * CHUNK]
            g_last = jnp.exp(gt[j * CHUNK:j * CHUNK + 1, :])
            state_ref[h] = st * g_last + _mm(k_dec[r], v_new, _TN)

        o = ointer_ref[...] + _mm(a_qk, vnew_ref[...])
        o = _rms_rows(o, ogain_ref[...])
        z = _mm(hb, wz_ref[:, h * A_DV:(h + 1) * A_DV])
        og_ref[:, h * A_DV:(h + 1) * A_DV] = (o * _silu(z)).astype(BF16)

    out_ref[0] = x + jnp.dot(og_ref[...], wout_ref[...], preferred_element_type=F32)


def _deltanet_layer(x, norm, w_in, conv_w, a_log, dt_bias, o_gain, w_out):
    b, t, d = x.shape
    tm = A_TILE
    w_qkv = w_in[:, :A_CONV_CH].astype(BF16)
    w_z = w_in[:, A_CONV_CH:A_CONV_CH + A_V_W].astype(BF16)
    w_ab = w_in[:, A_CONV_CH + A_V_W:].astype(BF16)
    in_specs = [
        pl.BlockSpec((1, tm, d), lambda bi, ti: (bi, ti, 0)),
        _const_spec((1, d)),
        _const_spec((d, A_CONV_CH)),
        _const_spec((d, A_V_W)),
        _const_spec((d, 2 * A_HEADS)),
        _const_spec((2 * A_HEADS, d)),
        _const_spec((CONV_W, A_CONV_CH)),
        _const_spec((1, A_HEADS)),
        _const_spec((1, A_HEADS)),
        _const_spec((A_HEADS, 1)),
        _const_spec((A_HEADS, 1)),
        _const_spec((1, A_DV)),
        _const_spec((A_V_W, d)),
    ]
    scratch = [
        pltpu.VMEM((tm + CONV_PAD, A_CONV_CH), F32),
        pltpu.VMEM((A_HEADS, A_DK, A_DV), F32),
        pltpu.VMEM((tm, A_QK_W), F32),
        pltpu.VMEM((tm, A_QK_W), F32),
        pltpu.VMEM((tm, A_V_W), F32),
        pltpu.VMEM((tm, A_V_W), BF16),
        pltpu.VMEM((tm, A_DV), F32),
        pltpu.VMEM((tm, A_DV), F32),
    ]
    return pl.pallas_call(
        _deltanet_kernel,
        out_shape=jax.ShapeDtypeStruct((b, t, d), F32),
        grid=(b, t // tm),
        in_specs=in_specs,
        out_specs=pl.BlockSpec((1, tm, d), lambda bi, ti: (bi, ti, 0)),
        scratch_shapes=scratch,
        compiler_params=pltpu.CompilerParams(
            dimension_semantics=("arbitrary", "arbitrary"), vmem_limit_bytes=VMEM_LIMIT),
        name="deltanet_layer",
    )(x, norm.reshape(1, d), w_qkv, w_z, w_ab, w_ab.T, conv_w,
      a_log.reshape(1, A_HEADS), dt_bias.reshape(1, A_HEADS),
      a_log.reshape(A_HEADS, 1), dt_bias.reshape(A_HEADS, 1),
      o_gain.reshape(1, A_DV), w_out.astype(BF16))


def _head_sum_matrix(width):
    r = lax.broadcasted_iota(jnp.int32, (width, width), 0) // B_HD
    c = lax.broadcasted_iota(jnp.int32, (width, width), 1) // B_HD
    return (r == c).astype(BF16)


def _head_rms(x, gain_tiled):
    width = x.shape[-1]
    summer = _head_sum_matrix(LANES)
    sq = x * x
    hi = sq.astype(BF16)
    lo = (sq - hi.astype(F32)).astype(BF16)
    parts = []
    for s in range(width // LANES):
        cols = slice(s * LANES, (s + 1) * LANES)
        parts.append(jnp.dot(hi[:, cols], summer, preferred_element_type=F32)
                     + jnp.dot(lo[:, cols], summer, preferred_element_type=F32))
    ss = jnp.concatenate(parts, axis=-1) if len(parts) > 1 else parts[0]
    return x * lax.rsqrt(ss * (1.0 / B_HD) + EPS) * gain_tiled


def _shared_kv_kernel(x_ref, norm_ref, wkv_ref, kgain_ref, k_ref, v_ref):
    hb = _rms_rows(x_ref[0], norm_ref[...]).astype(BF16)
    kv = jnp.dot(hb, wkv_ref[...], preferred_element_type=F32)
    k_ref[0] = _head_rms(kv[:, :B_KV_W], kgain_ref[...])
    v_ref[0] = kv[:, B_KV_W:]


def _shared_kv(x, kv_norm, w_kv, k_gain):
    b, t, d = x.shape
    tk = min(KV_TILE, t)
    out = jax.ShapeDtypeStruct((b, t, B_KV_W), F32)
    spec_out = pl.BlockSpec((1, tk, B_KV_W), lambda bi, ti: (bi, ti, 0))
    return pl.pallas_call(
        _shared_kv_kernel,
        out_shape=(out, out),
        grid=(b, t // tk),
        in_specs=[pl.BlockSpec((1, tk, d), lambda bi, ti: (bi, ti, 0)),
                  _const_spec((1, d)), _const_spec((d, 2 * B_KV_W)), _const_spec((1, B_KV_W))],
        out_specs=(spec_out, spec_out),
        compiler_params=pltpu.CompilerParams(
            dimension_semantics=("arbitrary", "arbitrary"), vmem_limit_bytes=VMEM_LIMIT),
        name="shared_kv",
    )(x, kv_norm.reshape(1, d), w_kv.astype(BF16), jnp.tile(k_gain, B_KV_HEADS).reshape(1, B_KV_W))


def _bucket_ranges():
    dist = np.arange(WINDOW)
    max_exact = N_BUCKETS // 2
    large = max_exact + (np.log(np.maximum(dist, 1) / max_exact) / np.log(MAX_DIST / max_exact)
                         * (N_BUCKETS - max_exact)).astype(np.int64)
    large = np.minimum(large, N_BUCKETS - 1)
    bucket = np.where(dist < max_exact, dist, large)
    ranges = []
    for bkt in range(N_BUCKETS):
        idx = np.nonzero(bucket == bkt)[0]
        if idx.size:
            assert np.all(np.diff(idx) == 1)
            ranges.append((bkt, int(idx[0]), int(idx[-1])))
    return ranges


def _band_bias_kernel(rel_ref, out_ref):
    h = pl.program_id(0)
    qi = lax.broadcasted_iota(jnp.int32, (BLOCK, 2 * BLOCK), 0)
    si = lax.broadcasted_iota(jnp.int32, (BLOCK, 2 * BLOCK), 1)
    dist = qi + BLOCK - si
    acc = jnp.full((BLOCK, 2 * BLOCK), NEG, F32)
    for bkt, lo, hi in _bucket_ranges():
        acc = jnp.where((dist >= lo) & (dist <= hi), rel_ref[bkt, h], acc)
    out_ref[0] = acc


def _band_bias(rel_bias):
    return pl.pallas_call(
        _band_bias_kernel,
        out_shape=jax.ShapeDtypeStruct((B_Q_HEADS, BLOCK, 2 * BLOCK), F32),
        grid=(B_Q_HEADS,),
        in_specs=[pl.BlockSpec(memory_space=pltpu.SMEM)],
        out_specs=pl.BlockSpec((1, BLOCK, 2 * BLOCK), lambda h: (h, 0, 0)),
        compiler_params=pltpu.CompilerParams(dimension_semantics=("arbitrary",)),
        name="band_bias",
    )(rel_bias)


def _swa_kernel(sinks_ref, x_ref, norm_ref, wq_ref, wz_ref, qgain_ref,
                kprev_ref, kcur_ref, vprev_ref, vcur_ref, bias_ref, wout_ref,
                out_ref,
                q_ref, ka_ref, kb_ref, va_ref, vb_ref, o_ref):
    tb = B_TILE
    first = pl.program_id(1) == 0
    x = x_ref[0]
    hb = _rms_rows(x, norm_ref[...]).astype(BF16)
    q = jnp.dot(hb, wq_ref[...], preferred_element_type=F32)
    q_ref[...] = _head_rms(q, qgain_ref[...] * (B_HD ** -0.5)).astype(BF16)

    lane = lax.broadcasted_iota(jnp.int32, (BLOCK + tb, LANES), 1)
    low_half = lane < B_HD
    for t2 in range(B_KV_W // LANES):
        cols = slice(t2 * LANES, (t2 + 1) * LANES)
        for src_prev, src_cur, dst_a, dst_b in ((kprev_ref, kcur_ref, ka_ref, kb_ref),
                                                (vprev_ref, vcur_ref, va_ref, vb_ref)):
            full = jnp.concatenate([src_prev[0, :, cols], src_cur[0, :, cols]], axis=0)
            swapped = pltpu.roll(full, B_HD, 1)
            zero = jnp.zeros_like(full)
            dst_a[2 * t2] = jnp.where(low_half, full, zero).astype(BF16)
            dst_b[2 * t2] = jnp.where(low_half, zero, swapped).astype(BF16)
            dst_a[2 * t2 + 1] = jnp.where(low_half, swapped, zero).astype(BF16)
            dst_b[2 * t2 + 1] = jnp.where(low_half, zero, full).astype(BF16)

    lane_q = lax.broadcasted_iota(jnp.int32, (BLOCK, LANES), 1)
    key_col = lax.broadcasted_iota(jnp.int32, (BLOCK, 2 * BLOCK), 1)
    for qb in range(tb // BLOCK):
        rows = slice(qb * BLOCK, (qb + 1) * BLOCK)
        band = slice(qb * BLOCK, qb * BLOCK + 2 * BLOCK)
        if qb == 0:
            pen = jnp.where(first & (key_col < BLOCK), NEG, 0.0)
        for p in range(B_Q_HEADS // 2):
            j = (2 * p) // B_GROUP
            qpair = q_ref[rows, p * LANES:(p + 1) * LANES]
            pv = None
            inv = []
            for e, (kz_ref, vz_ref) in enumerate(((ka_ref, va_ref), (kb_ref, vb_ref))):
                hq = 2 * p + e
                logits = lax.dot_general(qpair, kz_ref[j, band, :], _NT, preferred_element_type=F32)
                logits = logits + bias_ref[hq]
                if qb == 0:
                    logits = logits + pen
                sink = sinks_ref[hq]
                m = jnp.maximum(jnp.max(logits, axis=-1, keepdims=True), sink)
                pexp = jnp.exp(logits - m)
                denom = jnp.sum(pexp, axis=-1, keepdims=True) + jnp.exp(sink - m)
                inv.append(1.0 / denom)
                part = jnp.dot(pexp.astype(BF16), vz_ref[j, band, :], preferred_element_type=F32)
                pv = part if pv is None else pv + part
            o_ref[rows, p * LANES:(p + 1) * LANES] = pv * jnp.where(lane_q < B_HD, inv[0], inv[1])

    z = jnp.dot(hb, wz_ref[...], preferred_element_type=F32)
    og = (o_ref[...] * _silu(z)).astype(BF16)
    out_ref[0] = x + jnp.dot(og, wout_ref[...], preferred_element_type=F32)


def _swa_layer(x, norm, w_in, q_gain, sinks, w_out, k, v, bias):
    b, t, d = x.shape
    tb = B_TILE
    nblk = tb // BLOCK
    tile = lambda bi, ti: (bi, ti, 0)
    prev = lambda bi, ti: (bi, jnp.maximum(ti * nblk - 1, 0), 0)
    in_specs = [
        pl.BlockSpec(memory_space=pltpu.SMEM),
        pl.BlockSpec((1, tb, d), tile),
        _const_spec((1, d)),
        _const_spec((d, B_W)),
        _const_spec((d, B_W)),
        _const_spec((1, B_W)),
        pl.BlockSpec((1, BLOCK, B_KV_W), prev),
        pl.BlockSpec((1, tb, B_KV_W), tile),
        pl.BlockSpec((1, BLOCK, B_KV_W), prev),
        pl.BlockSpec((1, tb, B_KV_W), tile),
        _const_spec((B_Q_HEADS, BLOCK, 2 * BLOCK)),
        _const_spec((B_W, d)),
    ]
    kv_scratch = pltpu.VMEM((B_KV_HEADS, BLOCK + tb, LANES), BF16)
    scratch = [pltpu.VMEM((tb, B_W), BF16), kv_scratch, kv_scratch, kv_scratch, kv_scratch,
               pltpu.VMEM((tb, B_W), F32)]
    return pl.pallas_call(
        _swa_kernel,
        out_shape=jax.ShapeDtypeStruct((b, t, d), F32),
        grid=(b, t // tb),
        in_specs=in_specs,
        out_specs=pl.BlockSpec((1, tb, d), tile),
        scratch_shapes=scratch,
        compiler_params=pltpu.CompilerParams(
            dimension_semantics=("arbitrary", "arbitrary"), vmem_limit_bytes=VMEM_LIMIT),
        name="swa_layer",
    )(sinks, x, norm.reshape(1, d), w_in[:, :B_W].astype(BF16), w_in[:, B_W:].astype(BF16),
      jnp.tile(q_gain, B_Q_HEADS).reshape(1, B_W), k, k, v, v, bias, w_out.astype(BF16))


def kernel(x, a_norm, a_w_in, a_conv, a_A_log, a_dt_bias, a_o_gain, a_w_out, kv_norm, w_kv, k_gain, rel_bias, b_norm, b_w_in, b_q_gain, b_sinks, b_w_out):
    n_a = a_w_in.shape[0]
    n_b = b_w_in.shape[0]
    for i in range(n_a):
        x = _deltanet_layer(x, a_norm[i], a_w_in[i], a_conv[i], a_A_log[i], a_dt_bias[i],
                            a_o_gain[i], a_w_out[i])
    k, v = _shared_kv(x, kv_norm, w_kv, k_gain)
    bias = _band_bias(rel_bias)
    for j in range(n_b):
        x = _swa_layer(x, b_norm[j], b_w_in[j], b_q_gain[j], b_sinks[j], b_w_out[j], k, v, bias)
    return x
```

```python
import functools
import math

import numpy as np
import jax
import jax.numpy as jnp
from jax import lax
from jax.experimental import pallas as pl
from jax.experimental.pallas import tpu as pltpu

F32 = jnp.float32
BF16 = jnp.bfloat16

D_MODEL = 1024
EPS = 1e-6

A_HEADS = 8
A_DK = 128
A_DV = 256
A_QK_W = A_HEADS * A_DK
A_V_W = A_HEADS * A_DV
A_CONV_CH = 2 * A_QK_W + A_V_W
CONV_W = 4
CHUNK = 64
A_TILE = 256
CONV_PAD = 8
HEAD_GROUP = 8

B_Q_HEADS = 32
B_KV_HEADS = 4
B_GROUP = B_Q_HEADS // B_KV_HEADS
B_HD = 64
B_W = B_Q_HEADS * B_HD
B_KV_W = B_KV_HEADS * B_HD
WINDOW = 128
BLOCK = 128
B_TILE = 256
KV_TILE = 512
N_BUCKETS = 32
MAX_DIST = 128
NEG = -1e30

LANES = 128
VMEM_LIMIT = 56 * 1024 * 1024

_NT = (((1,), (1,)), ((), ()))
_NN = (((1,), (0,)), ((), ()))
_TN = (((0,), (0,)), ((), ()))


def _mm(a, b, dims=_NN):
    return lax.dot_general(a.astype(BF16), b.astype(BF16), dims, preferred_element_type=F32)


def _mm32(a, b, dims=_NN):
    return lax.dot_general(a, b, dims, precision=lax.Precision.HIGHEST, preferred_element_type=F32)


def _silu(x):
    return x * jax.nn.sigmoid(x)


def _rms_rows(x, gain):
    return x * lax.rsqrt(jnp.mean(x * x, axis=-1, keepdims=True) + EPS) * gain


def _const_spec(shape):
    nd = len(shape)
    return pl.BlockSpec(shape, lambda *_: (0,) * nd, pipeline_mode=pl.Buffered(1))


def _deltanet_kernel(x_ref, norm_ref, wqkv_ref, wz_ref, wab_ref, wabt_ref, conv_ref,
                     alog_ref, dtb_ref, alog_c_ref, dtb_c_ref, ogain_ref, wout_ref,
                     out_ref,
                     ext_ref, state_ref, q_ref, k_ref, v_ref, og_ref):
    tm = A_TILE
    n_chunks = tm // CHUNK
    log_c = int(math.log2(CHUNK))
    first = pl.program_id(1) == 0

    @pl.when(first)
    def _():
        ext_ref[0:CONV_PAD, :] = jnp.zeros((CONV_PAD, A_CONV_CH), F32)
        state_ref[...] = jnp.zeros_like(state_ref)

    x = x_ref[0]
    hb = _rms_rows(x, norm_ref[...]).astype(BF16)

    slab = 1024
    for s in range(A_CONV_CH // slab):
        ext_ref[CONV_PAD:CONV_PAD + tm, s * slab:(s + 1) * slab] = _mm(hb, wqkv_ref[:, s * slab:(s + 1) * slab])

    for s in range(A_CONV_CH // LANES):
        cols = slice(s * LANES, (s + 1) * LANES)
        acc = ext_ref[CONV_PAD:CONV_PAD + tm, cols] * conv_ref[CONV_W - 1:CONV_W, cols]
        for j in range(CONV_W - 1):
            off = CONV_PAD - (CONV_W - 1) + j
            acc = acc + ext_ref[off:off + tm, cols] * conv_ref[j:j + 1, cols]
        y = _silu(acc)
        if s < A_HEADS:
            y = y * (lax.rsqrt(jnp.sum(y * y, axis=-1, keepdims=True) + EPS) * (A_DK ** -0.5))
            q_ref[:, cols] = y
        elif s < 2 * A_HEADS:
            y = y * lax.rsqrt(jnp.sum(y * y, axis=-1, keepdims=True) + EPS)
            k_ref[:, s * LANES - A_QK_W:(s + 1) * LANES - A_QK_W] = y
        else:
            v_ref[:, s * LANES - 2 * A_QK_W:(s + 1) * LANES - 2 * A_QK_W] = y

    ext_ref[0:CONV_PAD, :] = ext_ref[tm:tm + CONV_PAD, :]

    ab = _mm(hb, wab_ref[...])
    ab_t = lax.dot_general(wabt_ref[...], hb, _NT, preferred_element_type=F32)
    beta_c = jax.nn.sigmoid(ab[:, 0:A_HEADS])
    g_c = -jnp.exp(alog_ref[...]) * jax.nn.softplus(ab[:, A_HEADS:2 * A_HEADS] + dtb_ref[...])
    g_r = -jnp.exp(alog_c_ref[...]) * jax.nn.softplus(ab_t[A_HEADS:2 * A_HEADS, :] + dtb_c_ref[...])

    ri = lax.broadcasted_iota(jnp.int32, (tm, tm), 0)
    ci = lax.broadcasted_iota(jnp.int32, (tm, tm), 1)
    same = (ri >> log_c) == (ci >> log_c)
    incl = same & (ri >= ci)
    strict = same & (ri > ci)
    eye = (ri == ci).astype(F32)
    gcum_c = _mm32(incl.astype(F32), g_c)
    gtot_c = _mm32(same.astype(F32), g_c)
    gcum_r = _mm32(g_r, (same & (ri <= ci)).astype(F32))

    for h0 in range(0, A_HEADS, HEAD_GROUP):
        heads = range(h0, h0 + HEAD_GROUP)
        a_qk, t_inv, pw = {}, {}, {}
        for h in heads:
            qh = q_ref[:, h * A_DK:(h + 1) * A_DK]
            kh = k_ref[:, h * A_DK:(h + 1) * A_DK]
            gc = gcum_c[:, h:h + 1]
            gr = gcum_r[h:h + 1, :]
            decay = jnp.where(incl, jnp.exp(jnp.minimum(gc - gr, 0.0)), 0.0)
            qk_kk = _mm(jnp.concatenate([qh, kh], axis=0), kh, _NT)
            a_qk[h] = (qk_kk[0:tm] * decay).astype(BF16)
            low = jnp.where(strict, qk_kk[tm:2 * tm] * decay * beta_c[:, h:h + 1], 0.0)
            t_inv[h] = eye - low
            pw[h] = low.astype(BF16)

        for rnd in range(log_c - 1):
            for h in heads:
                pw2 = jnp.dot(pw[h], pw[h], preferred_element_type=F32)
                pw[h] = pw2.astype(BF16)
                t_inv[h] = t_inv[h] + jnp.dot(t_inv[h].astype(BF16), pw[h], preferred_element_type=F32)

        o0, q_til, m_w, n_u = {}, {}, {}, {}
        for h in heads:
            qh = q_ref[:, h * A_DK:(h + 1) * A_DK]
            kh = k_ref[:, h * A_DK:(h + 1) * A_DK]
            vh = v_ref[:, h * A_DV:(h + 1) * A_DV]
            gc = gcum_c[:, h:h + 1]
            gt = gtot_c[:, h:h + 1]
            bc = beta_c[:, h:h + 1]
            eg = jnp.exp(gc)
            rhs = jnp.concatenate([(vh * bc).astype(BF16), (kh * (bc * eg)).astype(BF16)], axis=1)
            uw = jnp.dot(t_inv[h].astype(BF16), rhs, preferred_element_type=F32).astype(BF16)
            ao = jnp.dot(a_qk[h], uw, preferred_element_type=F32)
            o0[h] = ao[:, 0:A_DV]
            q_til[h] = (qh * eg - ao[:, A_DV:]).astype(BF16)
            k_dec = (kh * jnp.exp(gt - gc)).astype(BF16)
            for j in range(n_chunks):
                r = slice(j * CHUNK, (j + 1) * CHUNK)
                mn = lax.dot_general(k_dec[r], uw[r], _TN, preferred_element_type=F32)
                n_u[h, j] = mn[:, 0:A_DV]
                m_w[h, j] = mn[:, A_DV:].astype(BF16)

        st = {h: state_ref[h] for h in heads}
        o_parts = {h: [] for h in heads}
        for j in range(n_chunks):
            r = slice(j * CHUNK, (j + 1) * CHUNK)
            for h in heads:
                lhs = jnp.concatenate([m_w[h, j], q_til[h][r]], axis=0)
                ms = jnp.dot(lhs, st[h].astype(BF16), preferred_element_type=F32)
                o_parts[h].append(o0[h][r] + ms[A_DK:])
                g_last = jnp.exp(gtot_c[j * CHUNK:j * CHUNK + 1, h:h + 1])
                st[h] = st[h] * g_last - ms[0:A_DK] + n_u[h, j]

        for h in heads:
            state_ref[h] = st[h]
            o = _rms_rows(jnp.concatenate(o_parts[h], axis=0), ogain_ref[...])
            z = _mm(hb, wz_ref[:, h * A_DV:(h + 1) * A_DV])
            og_ref[:, h * A_DV:(h + 1) * A_DV] = (o * _silu(z)).astype(BF16)

    out_ref[0] = x + jnp.dot(og_ref[...], wout_ref[...], preferred_element_type=F32)


def _deltanet_layer(x, norm, w_in, conv_w, a_log, dt_bias, o_gain, w_out):
    b, t, d = x.shape
    tm = A_TILE
    w_qkv = w_in[:, :A_CONV_CH].astype(BF16)
    w_z = w_in[:, A_CONV_CH:A_CONV_CH + A_V_W].astype(BF16)
    w_ab = w_in[:, A_CONV_CH + A_V_W:].astype(BF16)
    in_specs = [
        pl.BlockSpec((1, tm, d), lambda bi, ti: (bi, ti, 0)),
        _const_spec((1, d)),
        _const_spec((d, A_CONV_CH)),
        _const_spec((d, A_V_W)),
        _const_spec((d, 2 * A_HEADS)),
        _const_spec((2 * A_HEADS, d)),
        _const_spec((CONV_W, A_CONV_CH)),
        _const_spec((1, A_HEADS)),
        _const_spec((1, A_HEADS)),
        _const_spec((A_HEADS, 1)),
        _const_spec((A_HEADS, 1)),
        _const_spec((1, A_DV)),
        _const_spec((A_V_W, d)),
    ]
    scratch = [
        pltpu.VMEM((tm + CONV_PAD, A_CONV_CH), F32),
        pltpu.VMEM((A_HEADS, A_DK, A_DV), F32),
        pltpu.VMEM((tm, A_QK_W), F32),
        pltpu.VMEM((tm, A_QK_W), F32),
        pltpu.VMEM((tm, A_V_W), F32),
        pltpu.VMEM((tm, A_V_W), BF16),
    ]
    return pl.pallas_call(
        _deltanet_kernel,
        out_shape=jax.ShapeDtypeStruct((b, t, d), F32),
        grid=(b, t // tm),
        in_specs=in_specs,
        out_specs=pl.BlockSpec((1, tm, d), lambda bi, ti: (bi, ti, 0)),
        scratch_shapes=scratch,
        compiler_params=pltpu.CompilerParams(
            dimension_semantics=("arbitrary", "arbitrary"), vmem_limit_bytes=VMEM_LIMIT),
        name="deltanet_layer",
    )(x, norm.reshape(1, d), w_qkv, w_z, w_ab, w_ab.T, conv_w,
      a_log.reshape(1, A_HEADS), dt_bias.reshape(1, A_HEADS),
      a_log.reshape(A_HEADS, 1), dt_bias.reshape(A_HEADS, 1),
      o_gain.reshape(1, A_DV), w_out.astype(BF16))


def _head_sum_matrix(width):
    r = lax.broadcasted_iota(jnp.int32, (width, width), 0) // B_HD
    c = lax.broadcasted_iota(jnp.int32, (width, width), 1) // B_HD
    return (r == c).astype(BF16)


def _head_rms(x, gain_tiled):
    width = x.shape[-1]
    summer = _head_sum_matrix(LANES)
    sq = x * x
    hi = sq.astype(BF16)
    lo = (sq - hi.astype(F32)).astype(BF16)
    parts = []
    for s in range(width // LANES):
        cols = slice(s * LANES, (s + 1) * LANES)
        parts.append(jnp.dot(hi[:, cols], summer, preferred_element_type=F32)
                     + jnp.dot(lo[:, cols], summer, preferred_element_type=F32))
    ss = jnp.concatenate(parts, axis=-1) if len(parts) > 1 else parts[0]
    return x * lax.rsqrt(ss * (1.0 / B_HD) + EPS) * gain_tiled


def _shared_kv_kernel(x_ref, norm_ref, wkv_ref, kgain_ref, k_ref, v_ref):
    hb = _rms_rows(x_ref[0], norm_ref[...]).astype(BF16)
    kv = jnp.dot(hb, wkv_ref[...], preferred_element_type=F32)
    k_ref[0] = _head_rms(kv[:, :B_KV_W], kgain_ref[...])
    v_ref[0] = kv[:, B_KV_W:]


def _shared_kv(x, kv_norm, w_kv, k_gain):
    b, t, d = x.shape
    tk = min(KV_TILE, t)
    out = jax.ShapeDtypeStruct((b, t, B_KV_W), F32)
    spec_out = pl.BlockSpec((1, tk, B_KV_W), lambda bi, ti: (bi, ti, 0))
    return pl.pallas_call(
        _shared_kv_kernel,
        out_shape=(out, out),
        grid=(b, t // tk),
        in_specs=[pl.BlockSpec((1, tk, d), lambda bi, ti: (bi, ti, 0)),
                  _const_spec((1, d)), _const_spec((d, 2 * B_KV_W)), _const_spec((1, B_KV_W))],
        out_specs=(spec_out, spec_out),
        compiler_params=pltpu.CompilerParams(
            dimension_semantics=("arbitrary", "arbitrary"), vmem_limit_bytes=VMEM_LIMIT),
        name="shared_kv",
    )(x, kv_norm.reshape(1, d), w_kv.astype(BF16), jnp.tile(k_gain, B_KV_HEADS).reshape(1, B_KV_W))


def _bucket_ranges():
    dist = np.arange(WINDOW)
    max_exact = N_BUCKETS // 2
    large = max_exact + (np.log(np.maximum(dist, 1) / max_exact) / np.log(MAX_DIST / max_exact)
                         * (N_BUCKETS - max_exact)).astype(np.int64)
    large = np.minimum(large, N_BUCKETS - 1)
    bucket = np.where(dist < max_exact, dist, large)
    ranges = []
    for bkt in range(N_BUCKETS):
        idx = np.nonzero(bucket == bkt)[0]
        if idx.size:
            assert np.all(np.diff(idx) == 1)
            ranges.append((bkt, int(idx[0]), int(idx[-1])))
    return ranges


def _band_bias_kernel(rel_ref, out_ref):
    h = pl.program_id(0)
    qi = lax.broadcasted_iota(jnp.int32, (BLOCK, 2 * BLOCK), 0)
    si = lax.broadcasted_iota(jnp.int32, (BLOCK, 2 * BLOCK), 1)
    dist = qi + BLOCK - si
    acc = jnp.full((BLOCK, 2 * BLOCK), NEG, F32)
    for bkt, lo, hi in _bucket_ranges():
        acc = jnp.where((dist >= lo) & (dist <= hi), rel_ref[bkt, h], acc)
    out_ref[0] = acc


def _band_bias(rel_bias):
    return pl.pallas_call(
        _band_bias_kernel,
        out_shape=jax.ShapeDtypeStruct((B_Q_HEADS, BLOCK, 2 * BLOCK), F32),
        grid=(B_Q_HEADS,),
        in_specs=[pl.BlockSpec(memory_space=pltpu.SMEM)],
        out_specs=pl.BlockSpec((1, BLOCK, 2 * BLOCK), lambda h: (h, 0, 0)),
        compiler_params=pltpu.CompilerParams(dimension_semantics=("arbitrary",)),
        name="band_bias",
    )(rel_bias)


def _swa_kernel(sinks_ref, x_ref, norm_ref, wq_ref, wz_ref, qgain_ref,
                kprev_ref, kcur_ref, vprev_ref, vcur_ref, bias_ref, wout_ref,
                out_ref,
                q_ref, ka_ref, kb_ref, va_ref, vb_ref, o_ref):
    tb = B_TILE
    first = pl.program_id(1) == 0
    x = x_ref[0]
    hb = _rms_rows(x, norm_ref[...]).astype(BF16)
    q = jnp.dot(hb, wq_ref[...], preferred_element_type=F32)
    q_ref[...] = _head_rms(q, qgain_ref[...] * (B_HD ** -0.5)).astype(BF16)

    lane = lax.broadcasted_iota(jnp.int32, (BLOCK + tb, LANES), 1)
    low_half = lane < B_HD
    for t2 in range(B_KV_W // LANES):
        cols = slice(t2 * LANES, (t2 + 1) * LANES)
        for src_prev, src_cur, dst_a, dst_b in ((kprev_ref, kcur_ref, ka_ref, kb_ref),
                                                (vprev_ref, vcur_ref, va_ref, vb_ref)):
            full = jnp.concatenate([src_prev[0, :, cols], src_cur[0, :, cols]], axis=0)
            swapped = pltpu.roll(full, B_HD, 1)
            zero = jnp.zeros_like(full)
            dst_a[2 * t2] = jnp.where(low_half, full, zero).astype(BF16)
            dst_b[2 * t2] = jnp.where(low_half, zero, swapped).astype(BF16)
            dst_a[2 * t2 + 1] = jnp.where(low_half, swapped, zero).astype(BF16)
            dst_b[2 * t2 + 1] = jnp.where(low_half, zero, full).astype(BF16)

    lane_q = lax.broadcasted_iota(jnp.int32, (BLOCK, LANES), 1)
    key_col = lax.broadcasted_iota(jnp.int32, (BLOCK, 2 * BLOCK), 1)
    for qb in range(tb // BLOCK):
        rows = slice(qb * BLOCK, (qb + 1) * BLOCK)
        band = slice(qb * BLOCK, qb * BLOCK + 2 * BLOCK)
        if qb == 0:
            pen = jnp.where(first & (key_col < BLOCK), NEG, 0.0)
        for p in range(B_Q_HEADS // 2):
            j = (2 * p) // B_GROUP
            qpair = q_ref[rows, p * LANES:(p + 1) * LANES]
            pv = None
            inv = []
            for e, (kz_ref, vz_ref) in enumerate(((ka_ref, va_ref), (kb_ref, vb_ref))):
                hq = 2 * p + e
                logits = lax.dot_general(qpair, kz_ref[j, band, :], _NT, preferred_element_type=F32)
                logits = logits + bias_ref[hq]
                if qb == 0:
                    logits = logits + pen
                sink = sinks_ref[hq]
                m = jnp.maximum(jnp.max(logits, axis=-1, keepdims=True), sink)
                pexp = jnp.exp(logits - m)
                denom = jnp.sum(pexp, axis=-1, keepdims=True) + jnp.exp(sink - m)
                inv.append(1.0 / denom)
                part = jnp.dot(pexp.astype(BF16), vz_ref[j, band, :], preferred_element_type=F32)
                pv = part if pv is None else pv + part
            o_ref[rows, p * LANES:(p + 1) * LANES] = pv * jnp.where(lane_q < B_HD, inv[0], inv[1])

    z = jnp.dot(hb, wz_ref[...], preferred_element_type=F32)
    og = (o_ref[...] * _silu(z)).astype(BF16)
    out_ref[0] = x + jnp.dot(og, wout_ref[...], preferred_element_type=F32)


def _swa_layer(x, norm, w_in, q_gain, sinks, w_out, k, v, bias):
    b, t, d = x.shape
    tb = B_TILE
    nblk = tb // BLOCK
    tile = lambda bi, ti: (bi, ti, 0)
    prev = lambda bi, ti: (bi, jnp.maximum(ti * nblk - 1, 0), 0)
    in_specs = [
        pl.BlockSpec(memory_space=pltpu.SMEM),
        pl.BlockSpec((1, tb, d), tile),
        _const_spec((1, d)),
        _const_spec((d, B_W)),
        _const_spec((d, B_W)),
        _const_spec((1, B_W)),
        pl.BlockSpec((1, BLOCK, B_KV_W), prev),
        pl.BlockSpec((1, tb, B_KV_W), tile),
        pl.BlockSpec((1, BLOCK, B_KV_W), prev),
        pl.BlockSpec((1, tb, B_KV_W), tile),
        _const_spec((B_Q_HEADS, BLOCK, 2 * BLOCK)),
        _const_spec((B_W, d)),
    ]
    kv_scratch = pltpu.VMEM((B_KV_HEADS, BLOCK + tb, LANES), BF16)
    scratch = [pltpu.VMEM((tb, B_W), BF16), kv_scratch, kv_scratch, kv_scratch, kv_scratch,
               pltpu.VMEM((tb, B_W), F32)]
    return pl.pallas_call(
        _swa_kernel,
        out_shape=jax.ShapeDtypeStruct((b, t, d), F32),
        grid=(b, t // tb),
        in_specs=in_specs,
        out_specs=pl.BlockSpec((1, tb, d), tile),
        scratch_shapes=scratch,
        compiler_params=pltpu.CompilerParams(
            dimension_semantics=("arbitrary", "arbitrary"), vmem_limit_bytes=VMEM_LIMIT),
        name="swa_layer",
    )(sinks, x, norm.reshape(1, d), w_in[:, :B_W].astype(BF16), w_in[:, B_W:].astype(BF16),
      jnp.tile(q_gain, B_Q_HEADS).reshape(1, B_W), k, k, v, v, bias, w_out.astype(BF16))


def kernel(x, a_norm, a_w_in, a_conv, a_A_log, a_dt_bias, a_o_gain, a_w_out, kv_norm, w_kv, k_gain, rel_bias, b_norm, b_w_in, b_q_gain, b_sinks, b_w_out):
    n_a = a_w_in.shape[0]
    n_b = b_w_in.shape[0]
    for i in range(n_a):
        x = _deltanet_layer(x, a_norm[i], a_w_in[i], a_conv[i], a_A_log[i], a_dt_bias[i],
                            a_o_gain[i], a_w_out[i])
    k, v = _shared_kv(x, kv_norm, w_kv, k_gain)
    bias = _band_bias(rel_bias)
    for j in range(n_b):
        x = _swa_layer(x, b_norm[j], b_w_in[j], b_q_gain[j], b_sinks[j], b_w_out[j], k, v, bias)
    return x
```

```python
import functools
import math

import numpy as np
import jax
import jax.numpy as jnp
from jax import lax
from jax.experimental import pallas as pl
from jax.experimental.pallas import tpu as pltpu

F32 = jnp.float32
BF16 = jnp.bfloat16

D_MODEL = 1024
EPS = 1e-6

A_HEADS = 8
A_DK = 128
A_DV = 256
A_QK_W = A_HEADS * A_DK
A_V_W = A_HEADS * A_DV
A_CONV_CH = 2 * A_QK_W + A_V_W
CONV_W = 4
CHUNK = 64
A_TILE = 256
CONV_PAD = 8
HEAD_GROUP = 8

B_Q_HEADS = 32
B_KV_HEADS = 4
B_GROUP = B_Q_HEADS // B_KV_HEADS
B_HD = 64
B_W = B_Q_HEADS * B_HD
B_KV_W = B_KV_HEADS * B_HD
WINDOW = 128
BLOCK = 128
B_TILE = 256
SWA_HEAD_GROUP = 8
KV_TILE = 512
N_BUCKETS = 32
MAX_DIST = 128
NEG = -1e30

LANES = 128
VMEM_LIMIT = 56 * 1024 * 1024

_NT = (((1,), (1,)), ((), ()))
_NN = (((1,), (0,)), ((), ()))
_TN = (((0,), (0,)), ((), ()))


def _mm(a, b, dims=_NN):
    return lax.dot_general(a.astype(BF16), b.astype(BF16), dims, preferred_element_type=F32)


def _mm32(a, b, dims=_NN):
    return lax.dot_general(a, b, dims, precision=lax.Precision.HIGHEST, preferred_element_type=F32)


def _silu(x):
    return x * jax.nn.sigmoid(x)


def _rms_rows(x, gain):
    return x * lax.rsqrt(jnp.mean(x * x, axis=-1, keepdims=True) + EPS) * gain


def _const_spec(shape):
    nd = len(shape)
    return pl.BlockSpec(shape, lambda *_: (0,) * nd, pipeline_mode=pl.Buffered(1))


def _deltanet_kernel(x_ref, norm_ref, wqkv_ref, wz_ref, wab_ref, wabt_ref, conv_ref,
                     alog_ref, dtb_ref, alog_c_ref, dtb_c_ref, ogain_ref, wout_ref,
                     out_ref,
                     ext_ref, state_ref, q_ref, k_ref, v_ref, og_ref):
    tm = A_TILE
    n_chunks = tm // CHUNK
    log_c = int(math.log2(CHUNK))
    first = pl.program_id(1) == 0

    @pl.when(first)
    def _():
        ext_ref[0:CONV_PAD, :] = jnp.zeros((CONV_PAD, A_CONV_CH), F32)
        state_ref[...] = jnp.zeros_like(state_ref)

    x = x_ref[0]
    hb = _rms_rows(x, norm_ref[...]).astype(BF16)

    slab = 1024
    for s in range(A_CONV_CH // slab):
        ext_ref[CONV_PAD:CONV_PAD + tm, s * slab:(s + 1) * slab] = _mm(hb, wqkv_ref[:, s * slab:(s + 1) * slab])

    for s in range(A_CONV_CH // LANES):
        cols = slice(s * LANES, (s + 1) * LANES)
        acc = ext_ref[CONV_PAD:CONV_PAD + tm, cols] * conv_ref[CONV_W - 1:CONV_W, cols]
        for j in range(CONV_W - 1):
            off = CONV_PAD - (CONV_W - 1) + j
            acc = acc + ext_ref[off:off + tm, cols] * conv_ref[j:j + 1, cols]
        y = _silu(acc)
        if s < A_HEADS:
            y = y * (lax.rsqrt(jnp.sum(y * y, axis=-1, keepdims=True) + EPS) * (A_DK ** -0.5))
            q_ref[:, cols] = y
        elif s < 2 * A_HEADS:
            y = y * lax.rsqrt(jnp.sum(y * y, axis=-1, keepdims=True) + EPS)
            k_ref[:, s * LANES - A_QK_W:(s + 1) * LANES - A_QK_W] = y
        else:
            v_ref[:, s * LANES - 2 * A_QK_W:(s + 1) * LANES - 2 * A_QK_W] = y

    ext_ref[0:CONV_PAD, :] = ext_ref[tm:tm + CONV_PAD, :]

    ab = _mm(hb, wab_ref[...])
    ab_t = lax.dot_general(wabt_ref[...], hb, _NT, preferred_element_type=F32)
    beta_c = jax.nn.sigmoid(ab[:, 0:A_HEADS])
    g_c = -jnp.exp(alog_ref[...]) * jax.nn.softplus(ab[:, A_HEADS:2 * A_HEADS] + dtb_ref[...])
    g_r = -jnp.exp(alog_c_ref[...]) * jax.nn.softplus(ab_t[A_HEADS:2 * A_HEADS, :] + dtb_c_ref[...])

    ri = lax.broadcasted_iota(jnp.int32, (tm, tm), 0)
    ci = lax.broadcasted_iota(jnp.int32, (tm, tm), 1)
    same = (ri >> log_c) == (ci >> log_c)
    incl = same & (ri >= ci)
    strict = same & (ri > ci)
    eye = (ri == ci).astype(F32)
    gcum_c = _mm32(incl.astype(F32), g_c)
    gtot_c = _mm32(same.astype(F32), g_c)
    gcum_r = _mm32(g_r, (same & (ri <= ci)).astype(F32))

    for h0 in range(0, A_HEADS, HEAD_GROUP):
        heads = range(h0, h0 + HEAD_GROUP)
        a_qk, t_inv, pw = {}, {}, {}
        for h in heads:
            qh = q_ref[:, h * A_DK:(h + 1) * A_DK]
            kh = k_ref[:, h * A_DK:(h + 1) * A_DK]
            gc = gcum_c[:, h:h + 1]
            gr = gcum_r[h:h + 1, :]
            decay = jnp.where(incl, jnp.exp(jnp.minimum(gc - gr, 0.0)), 0.0)
            qk_kk = _mm(jnp.concatenate([qh, kh], axis=0), kh, _NT)
            a_qk[h] = (qk_kk[0:tm] * decay).astype(BF16)
            low = jnp.where(strict, qk_kk[tm:2 * tm] * decay * beta_c[:, h:h + 1], 0.0)
            t_inv[h] = eye - low
            pw[h] = low.astype(BF16)

        for rnd in range(log_c - 1):
            for h in heads:
                pw2 = jnp.dot(pw[h], pw[h], preferred_element_type=F32)
                pw[h] = pw2.astype(BF16)
                t_inv[h] = t_inv[h] + jnp.dot(t_inv[h].astype(BF16), pw[h], preferred_element_type=F32)

        o0, q_til, m_w, n_u = {}, {}, {}, {}
        for h in heads:
            qh = q_ref[:, h * A_DK:(h + 1) * A_DK]
            kh = k_ref[:, h * A_DK:(h + 1) * A_DK]
            vh = v_ref[:, h * A_DV:(h + 1) * A_DV]
            gc = gcum_c[:, h:h + 1]
            gt = gtot_c[:, h:h + 1]
            bc = beta_c[:, h:h + 1]
            eg = jnp.exp(gc)
            rhs = jnp.concatenate([(vh * bc).astype(BF16), (kh * (bc * eg)).astype(BF16)], axis=1)
            uw = jnp.dot(t_inv[h].astype(BF16), rhs, preferred_element_type=F32).astype(BF16)
            ao = jnp.dot(a_qk[h], uw, preferred_element_type=F32)
            o0[h] = ao[:, 0:A_DV]
            q_til[h] = (qh * eg - ao[:, A_DV:]).astype(BF16)
            k_dec = (kh * jnp.exp(gt - gc)).astype(BF16)
            for j in range(n_chunks):
                r = slice(j * CHUNK, (j + 1) * CHUNK)
                mn = lax.dot_general(k_dec[r], uw[r], _TN, preferred_element_type=F32)
                n_u[h, j] = mn[:, 0:A_DV]
                m_w[h, j] = mn[:, A_DV:].astype(BF16)

        st = {h: state_ref[h] for h in heads}
        o_parts = {h: [] for h in heads}
        for j in range(n_chunks):
            r = slice(j * CHUNK, (j + 1) * CHUNK)
            for h in heads:
                lhs = jnp.concatenate([m_w[h, j], q_til[h][r]], axis=0)
                ms = jnp.dot(lhs, st[h].astype(BF16), preferred_element_type=F32)
                o_parts[h].append(o0[h][r] + ms[A_DK:])
                g_last = jnp.exp(gtot_c[j * CHUNK:j * CHUNK + 1, h:h + 1])
                st[h] = st[h] * g_last - ms[0:A_DK] + n_u[h, j]

        for h in heads:
            state_ref[h] = st[h]
            o = _rms_rows(jnp.concatenate(o_parts[h], axis=0), ogain_ref[...])
            z = _mm(hb, wz_ref[:, h * A_DV:(h + 1) * A_DV])
            og_ref[:, h * A_DV:(h + 1) * A_DV] = (o * _silu(z)).astype(BF16)

    out_ref[0] = x + jnp.dot(og_ref[...], wout_ref[...], preferred_element_type=F32)


def _deltanet_layer(x, norm, w_in, conv_w, a_log, dt_bias, o_gain, w_out):
    b, t, d = x.shape
    tm = A_TILE
    w_qkv = w_in[:, :A_CONV_CH].astype(BF16)
    w_z = w_in[:, A_CONV_CH:A_CONV_CH + A_V_W].astype(BF16)
    w_ab = w_in[:, A_CONV_CH + A_V_W:].astype(BF16)
    in_specs = [
        pl.BlockSpec((1, tm, d), lambda bi, ti: (bi, ti, 0)),
        _const_spec((1, d)),
        _const_spec((d, A_CONV_CH)),
        _const_spec((d, A_V_W)),
        _const_spec((d, 2 * A_HEADS)),
        _const_spec((2 * A_HEADS, d)),
        _const_spec((CONV_W, A_CONV_CH)),
        _const_spec((1, A_HEADS)),
        _const_spec((1, A_HEADS)),
        _const_spec((A_HEADS, 1)),
        _const_spec((A_HEADS, 1)),
        _const_spec((1, A_DV)),
        _const_spec((A_V_W, d)),
    ]
    scratch = [
        pltpu.VMEM((tm + CONV_PAD, A_CONV_CH), F32),
        pltpu.VMEM((A_HEADS, A_DK, A_DV), F32),
        pltpu.VMEM((tm, A_QK_W), F32),
        pltpu.VMEM((tm, A_QK_W), F32),
        pltpu.VMEM((tm, A_V_W), F32),
        pltpu.VMEM((tm, A_V_W), BF16),
    ]
    return pl.pallas_call(
        _deltanet_kernel,
        out_shape=jax.ShapeDtypeStruct((b, t, d), F32),
        grid=(b, t // tm),
        in_specs=in_specs,
        out_specs=pl.BlockSpec((1, tm, d), lambda bi, ti: (bi, ti, 0)),
        scratch_shapes=scratch,
        compiler_params=pltpu.CompilerParams(
            dimension_semantics=("arbitrary", "arbitrary"), vmem_limit_bytes=VMEM_LIMIT),
        name="deltanet_layer",
    )(x, norm.reshape(1, d), w_qkv, w_z, w_ab, w_ab.T, conv_w,
      a_log.reshape(1, A_HEADS), dt_bias.reshape(1, A_HEADS),
      a_log.reshape(A_HEADS, 1), dt_bias.reshape(A_HEADS, 1),
      o_gain.reshape(1, A_DV), w_out.astype(BF16))


def _head_sum_matrix(width):
    r = lax.broadcasted_iota(jnp.int32, (width, width), 0) // B_HD
    c = lax.broadcasted_iota(jnp.int32, (width, width), 1) // B_HD
    return (r == c).astype(BF16)


def _head_rms(x, gain_tiled):
    width = x.shape[-1]
    summer = _head_sum_matrix(LANES)
    sq = x * x
    hi = sq.astype(BF16)
    lo = (sq - hi.astype(F32)).astype(BF16)
    parts = []
    for s in range(width // LANES):
        cols = slice(s * LANES, (s + 1) * LANES)
        parts.append(jnp.dot(hi[:, cols], summer, preferred_element_type=F32)
                     + jnp.dot(lo[:, cols], summer, preferred_element_type=F32))
    ss = jnp.concatenate(parts, axis=-1) if len(parts) > 1 else parts[0]
    return x * lax.rsqrt(ss * (1.0 / B_HD) + EPS) * gain_tiled


def _shared_kv_kernel(x_ref, norm_ref, wkv_ref, kgain_ref, k_ref, v_ref):
    hb = _rms_rows(x_ref[0], norm_ref[...]).astype(BF16)
    kv = jnp.dot(hb, wkv_ref[...], preferred_element_type=F32)
    k_ref[0] = _head_rms(kv[:, :B_KV_W], kgain_ref[...])
    v_ref[0] = kv[:, B_KV_W:]


def _shared_kv(x, kv_norm, w_kv, k_gain):
    b, t, d = x.shape
    tk = min(KV_TILE, t)
    out = jax.ShapeDtypeStruct((b, t, B_KV_W), F32)
    spec_out = pl.BlockSpec((1, tk, B_KV_W), lambda bi, ti: (bi, ti, 0))
    return pl.pallas_call(
        _shared_kv_kernel,
        out_shape=(out, out),
        grid=(b, t // tk),
        in_specs=[pl.BlockSpec((1, tk, d), lambda bi, ti: (bi, ti, 0)),
                  _const_spec((1, d)), _const_spec((d, 2 * B_KV_W)), _const_spec((1, B_KV_W))],
        out_specs=(spec_out, spec_out),
        compiler_params=pltpu.CompilerParams(
            dimension_semantics=("arbitrary", "arbitrary"), vmem_limit_bytes=VMEM_LIMIT),
        name="shared_kv",
    )(x, kv_norm.reshape(1, d), w_kv.astype(BF16), jnp.tile(k_gain, B_KV_HEADS).reshape(1, B_KV_W))


def _bucket_ranges():
    dist = np.arange(WINDOW)
    max_exact = N_BUCKETS // 2
    large = max_exact + (np.log(np.maximum(dist, 1) / max_exact) / np.log(MAX_DIST / max_exact)
                         * (N_BUCKETS - max_exact)).astype(np.int64)
    large = np.minimum(large, N_BUCKETS - 1)
    bucket = np.where(dist < max_exact, dist, large)
    ranges = []
    for bkt in range(N_BUCKETS):
        idx = np.nonzero(bucket == bkt)[0]
        if idx.size:
            assert np.all(np.diff(idx) == 1)
            ranges.append((bkt, int(idx[0]), int(idx[-1])))
    return ranges


def _band_bias_kernel(rel_ref, out_ref):
    h = pl.program_id(0)
    qi = lax.broadcasted_iota(jnp.int32, (BLOCK, 2 * BLOCK), 0)
    si = lax.broadcasted_iota(jnp.int32, (BLOCK, 2 * BLOCK), 1)
    dist = qi + BLOCK - si
    acc = jnp.full((BLOCK, 2 * BLOCK), NEG, F32)
    for bkt, lo, hi in _bucket_ranges():
        acc = jnp.where((dist >= lo) & (dist <= hi), rel_ref[bkt, h], acc)
    out_ref[0] = acc


def _band_bias(rel_bias):
    return pl.pallas_call(
        _band_bias_kernel,
        out_shape=jax.ShapeDtypeStruct((B_Q_HEADS, BLOCK, 2 * BLOCK), F32),
        grid=(B_Q_HEADS,),
        in_specs=[pl.BlockSpec(memory_space=pltpu.SMEM)],
        out_specs=pl.BlockSpec((1, BLOCK, 2 * BLOCK), lambda h: (h, 0, 0)),
        compiler_params=pltpu.CompilerParams(dimension_semantics=("arbitrary",)),
        name="band_bias",
    )(rel_bias)


def _swa_kernel(sinks_ref, x_ref, norm_ref, wq_ref, wz_ref, qgain_ref,
                kprev_ref, kcur_ref, vprev_ref, vcur_ref, bias_ref, wout_ref,
                out_ref,
                q_ref, ka_ref, kb_ref, va_ref, vb_ref, o_ref):
    tb = B_TILE
    first = pl.program_id(1) == 0
    x = x_ref[0]
    hb = _rms_rows(x, norm_ref[...]).astype(BF16)
    q = jnp.dot(hb, wq_ref[...], preferred_element_type=F32)
    q_ref[...] = _head_rms(q, qgain_ref[...] * (B_HD ** -0.5)).astype(BF16)

    lane = lax.broadcasted_iota(jnp.int32, (BLOCK + tb, LANES), 1)
    low_half = lane < B_HD
    for t2 in range(B_KV_W // LANES):
        cols = slice(t2 * LANES, (t2 + 1) * LANES)
        for src_prev, src_cur, dst_a, dst_b in ((kprev_ref, kcur_ref, ka_ref, kb_ref),
                                                (vprev_ref, vcur_ref, va_ref, vb_ref)):
            full = jnp.concatenate([src_prev[0, :, cols], src_cur[0, :, cols]], axis=0)
            swapped = pltpu.roll(full, B_HD, 1)
            zero = jnp.zeros_like(full)
            dst_a[2 * t2] = jnp.where(low_half, full, zero).astype(BF16)
            dst_b[2 * t2] = jnp.where(low_half, zero, swapped).astype(BF16)
            dst_a[2 * t2 + 1] = jnp.where(low_half, swapped, zero).astype(BF16)
            dst_b[2 * t2 + 1] = jnp.where(low_half, zero, full).astype(BF16)

    lane_q = lax.broadcasted_iota(jnp.int32, (BLOCK, LANES), 1)
    key_col = lax.broadcasted_iota(jnp.int32, (BLOCK, 2 * BLOCK), 1)
    for qb in range(tb // BLOCK):
        rows = slice(qb * BLOCK, (qb + 1) * BLOCK)
        band = slice(qb * BLOCK, qb * BLOCK + 2 * BLOCK)
        if qb == 0:
            pen = jnp.where(first & (key_col < BLOCK), NEG, 0.0)
        for h0 in range(0, B_Q_HEADS, SWA_HEAD_GROUP):
            heads = range(h0, h0 + SWA_HEAD_GROUP)
            j = h0 // B_GROUP
            kz = (ka_ref[j, band, :], kb_ref[j, band, :])
            vz = (va_ref[j, band, :], vb_ref[j, band, :])
            logits, m, pexp, inv = {}, {}, {}, {}
            for hq in heads:
                qpair = q_ref[rows, (hq // 2) * LANES:(hq // 2 + 1) * LANES]
                lg = lax.dot_general(qpair, kz[hq % 2], _NT, preferred_element_type=F32) + bias_ref[hq]
                logits[hq] = lg + pen if qb == 0 else lg
            for hq in heads:
                m[hq] = jnp.maximum(jnp.max(logits[hq], axis=-1, keepdims=True), sinks_ref[hq])
            for hq in heads:
                pe = jnp.exp(logits[hq] - m[hq])
                denom = jnp.sum(pe, axis=-1, keepdims=True) + jnp.exp(sinks_ref[hq] - m[hq])
                inv[hq] = 1.0 / denom
                pexp[hq] = pe.astype(BF16)
            for hq in heads[::2]:
                pv = (jnp.dot(pexp[hq], vz[0], preferred_element_type=F32)
                      + jnp.dot(pexp[hq + 1], vz[1], preferred_element_type=F32))
                p = hq // 2
                o_ref[rows, p * LANES:(p + 1) * LANES] = pv * jnp.where(lane_q < B_HD, inv[hq], inv[hq + 1])

    z = jnp.dot(hb, wz_ref[...], preferred_element_type=F32)
    og = (o_ref[...] * _silu(z)).astype(BF16)
    out_ref[0] = x + jnp.dot(og, wout_ref[...], preferred_element_type=F32)


def _swa_layer(x, norm, w_in, q_gain, sinks, w_out, k, v, bias):
    b, t, d = x.shape
    tb = B_TILE
    nblk = tb // BLOCK
    tile = lambda bi, ti: (bi, ti, 0)
    prev = lambda bi, ti: (bi, jnp.maximum(ti * nblk - 1, 0), 0)
    in_specs = [
        pl.BlockSpec(memory_space=pltpu.SMEM),
        pl.BlockSpec((1, tb, d), tile),
        _const_spec((1, d)),
        _const_spec((d, B_W)),
        _const_spec((d, B_W)),
        _const_spec((1, B_W)),
        pl.BlockSpec((1, BLOCK, B_KV_W), prev),
        pl.BlockSpec((1, tb, B_KV_W), tile),
        pl.BlockSpec((1, BLOCK, B_KV_W), prev),
        pl.BlockSpec((1, tb, B_KV_W), tile),
        _const_spec((B_Q_HEADS, BLOCK, 2 * BLOCK)),
        _const_spec((B_W, d)),
    ]
    kv_scratch = pltpu.VMEM((B_KV_HEADS, BLOCK + tb, LANES), BF16)
    scratch = [pltpu.VMEM((tb, B_W), BF16), kv_scratch, kv_scratch, kv_scratch, kv_scratch,
               pltpu.VMEM((tb, B_W), F32)]
    return pl.pallas_call(
        _swa_kernel,
        out_shape=jax.ShapeDtypeStruct((b, t, d), F32),
        grid=(b, t // tb),
        in_specs=in_specs,
        out_specs=pl.BlockSpec((1, tb, d), tile),
        scratch_shapes=scratch,
        compiler_params=pltpu.CompilerParams(
            dimension_semantics=("arbitrary", "arbitrary"), vmem_limit_bytes=VMEM_LIMIT),
        name="swa_layer",
    )(sinks, x, norm.reshape(1, d), w_in[:, :B_W].astype(BF16), w_in[:, B_W:].astype(BF16),
      jnp.tile(q_gain, B_Q_HEADS).reshape(1, B_W), k, k, v, v, bias, w_out.astype(BF16))


def kernel(x, a_norm, a_w_in, a_conv, a_A_log, a_dt_bias, a_o_gain, a_w_out, kv_norm, w_kv, k_gain, rel_bias, b_norm, b_w_in, b_q_gain, b_sinks, b_w_out):
    n_a = a_w_in.shape[0]
    n_b = b_w_in.shape[0]
    for i in range(n_a):
        x = _deltanet_layer(x, a_norm[i], a_w_in[i], a_conv[i], a_A_log[i], a_dt_bias[i],
                            a_o_gain[i], a_w_out[i])
    k, v = _shared_kv(x, kv_norm, w_kv, k_gain)
    bias = _band_bias(rel_bias)
    for j in range(n_b):
        x = _swa_layer(x, b_norm[j], b_w_in[j], b_q_gain[j], b_sinks[j], b_w_out[j], k, v, bias)
    return x
```

```python
import functools
import math

import numpy as np
import jax
import jax.numpy as jnp
from jax import lax
from jax.experimental import pallas as pl
from jax.experimental.pallas import tpu as pltpu

F32 = jnp.float32
BF16 = jnp.bfloat16

D_MODEL = 1024
EPS = 1e-6

A_HEADS = 8
A_DK = 128
A_DV = 256
A_QK_W = A_HEADS * A_DK
A_V_W = A_HEADS * A_DV
A_CONV_CH = 2 * A_QK_W + A_V_W
CONV_W = 4
CHUNK = 64
A_TILE = 256
CONV_PAD = 8
SBS_HEADS = 4

B_Q_HEADS = 32
B_KV_HEADS = 4
B_GROUP = B_Q_HEADS // B_KV_HEADS
B_HD = 64
B_W = B_Q_HEADS * B_HD
B_KV_W = B_KV_HEADS * B_HD
WINDOW = 128
BLOCK = 128
B_TILE = 256
SWA_HEAD_GROUP = 8
KV_TILE = 512
N_BUCKETS = 32
MAX_DIST = 128
NEG = -1e30

LANES = 128
VMEM_LIMIT = 56 * 1024 * 1024

_NT = (((1,), (1,)), ((), ()))
_NN = (((1,), (0,)), ((), ()))
_TN = (((0,), (0,)), ((), ()))


def _mm(a, b, dims=_NN):
    return lax.dot_general(a.astype(BF16), b.astype(BF16), dims, preferred_element_type=F32)


def _mm32(a, b, dims=_NN):
    return lax.dot_general(a, b, dims, precision=lax.Precision.HIGHEST, preferred_element_type=F32)


def _silu(x):
    return x * jax.nn.sigmoid(x)


def _rms_rows(x, gain):
    return x * lax.rsqrt(jnp.mean(x * x, axis=-1, keepdims=True) + EPS) * gain


def _const_spec(shape):
    nd = len(shape)
    return pl.BlockSpec(shape, lambda *_: (0,) * nd, pipeline_mode=pl.Buffered(1))


def _deltanet_kernel(x_ref, norm_ref, wqkv_ref, wz_ref, wab_ref, wabt_ref, conv_ref,
                     alog_ref, dtb_ref, alog_c_ref, dtb_c_ref, ogain_ref, wout_ref,
                     out_ref,
                     ext_ref, state_ref, q_ref, k_ref, v_ref, gate_ref, og_ref):
    tm = A_TILE
    c = CHUNK
    sbs = SBS_HEADS
    n_chunks = tm // c
    log_c = int(math.log2(c))
    assert n_chunks == sbs
    first = pl.program_id(1) == 0

    @pl.when(first)
    def _():
        ext_ref[0:CONV_PAD, :] = jnp.zeros((CONV_PAD, A_CONV_CH), F32)
        state_ref[...] = jnp.zeros_like(state_ref)

    x = x_ref[0]
    hb = _rms_rows(x, norm_ref[...]).astype(BF16)


    def project(lo, hi):
        ext_ref[CONV_PAD:CONV_PAD + tm, lo:hi] = _mm(hb, wqkv_ref[:, lo:hi])

    def conv_slab(s):
        cols = slice(s * LANES, (s + 1) * LANES)
        ext = ext_ref[:, cols]
        acc = ext * conv_ref[0:1, cols]
        for j in range(1, CONV_W):
            acc = ext * conv_ref[j:j + 1, cols] + pltpu.roll(acc, 1, 0)
        y = _silu(acc[CONV_PAD:CONV_PAD + tm])
        if s < A_HEADS:
            y = y * (lax.rsqrt(jnp.sum(y * y, axis=-1, keepdims=True) + EPS) * (A_DK ** -0.5))
            q_ref[:, cols] = y
        elif s < 2 * A_HEADS:
            y = y * lax.rsqrt(jnp.sum(y * y, axis=-1, keepdims=True) + EPS)
            k_ref[:, s * LANES - A_QK_W:(s + 1) * LANES - A_QK_W] = y
        else:
            v_ref[:, s * LANES - 2 * A_QK_W:(s + 1) * LANES - 2 * A_QK_W] = y

    def gate_proj(h):
        gate_ref[:, h * A_DV:(h + 1) * A_DV] = _silu(_mm(hb, wz_ref[:, h * A_DV:(h + 1) * A_DV]))

    slab = 512
    for lo in range(0, 2 * A_QK_W, slab):
        project(lo, lo + slab)

    ab = _mm(hb, wab_ref[...])
    ab_t = lax.dot_general(wabt_ref[...], hb, _NT, preferred_element_type=F32)
    beta_c = jax.nn.sigmoid(ab[:, 0:A_HEADS])
    g_c = -jnp.exp(alog_ref[...]) * jax.nn.softplus(ab[:, A_HEADS:2 * A_HEADS] + dtb_ref[...])
    g_r = -jnp.exp(alog_c_ref[...]) * jax.nn.softplus(ab_t[A_HEADS:2 * A_HEADS, :] + dtb_c_ref[...])

    ri = lax.broadcasted_iota(jnp.int32, (tm, tm), 0)
    ci = lax.broadcasted_iota(jnp.int32, (tm, tm), 1)
    same = (ri >> log_c) == (ci >> log_c)
    gcum_c = _mm32((same & (ri >= ci)).astype(F32), g_c)
    gtot_c = _mm32(same.astype(F32), g_c)
    gcum_r = _mm32(g_r, (same & (ri <= ci)).astype(F32))
    gcum_rot = [gcum_r] + [pltpu.roll(gcum_r, k * c, 1) for k in range(1, n_chunks)]

    for s in range(2 * A_HEADS):
        conv_slab(s)
    for lo in range(2 * A_QK_W, A_CONV_CH, slab):
        project(lo, lo + slab)

    sw = sbs * c
    row_i = lax.broadcasted_iota(jnp.int32, (c, sw), 0)
    lane_i = lax.broadcasted_iota(jnp.int32, (c, sw), 1)
    pos_i = lane_i & (c - 1)
    blk_i = lane_i >> log_c
    incl = row_i >= pos_i
    strict = row_i > pos_i
    eye = (row_i == pos_i).astype(F32)
    bd_p = ((lax.broadcasted_iota(jnp.int32, (sw, sw), 0) >> log_c)
            == (lax.broadcasted_iota(jnp.int32, (sw, sw), 1) >> log_c))
    bd_k = ((lax.broadcasted_iota(jnp.int32, (sw, sbs * A_DK), 0) >> log_c)
            == (lax.broadcasted_iota(jnp.int32, (sw, sbs * A_DK), 1) >> int(math.log2(A_DK))))

    def spread(cols):
        out = cols[:, sbs - 1:sbs]
        for hp in range(sbs - 2, -1, -1):
            out = jnp.where(blk_i == hp, cols[:, hp:hp + 1], out)
        return out

    def block_diag(m, mask):
        return jnp.where(mask, jnp.concatenate([m] * sbs, axis=0), jnp.zeros((), m.dtype))

    units =[(g, j) for g in range(A_HEADS // sbs) for j in range(n_chunks)]
    a_qk, t_inv, pw = {}, {}, {}
    for g, j in units:
        rows = slice(j * c, (j + 1) * c)
        hcols = slice(g * sbs * A_DK, (g + 1) * sbs * A_DK)
        qg = q_ref[rows, hcols].astype(BF16)
        kg = k_ref[rows, hcols].astype(BF16)
        qk_kk = lax.dot_general(jnp.concatenate([qg, kg], axis=0), block_diag(kg, bd_k), _NT,
                                preferred_element_type=F32)
        gcol = spread(gcum_c[rows, g * sbs:(g + 1) * sbs])
        grow = gcum_rot[(sbs - 1 - j) % n_chunks][g * sbs + sbs - 1:g * sbs + sbs, :]
        for hp in range(sbs - 2, -1, -1):
            grow = jnp.where(blk_i[0:1] == hp,
                             gcum_rot[(hp - j) % n_chunks][g * sbs + hp:g * sbs + hp + 1, :], grow)
        decay = jnp.where(incl, jnp.exp(jnp.minimum(gcol - grow, 0.0)), 0.0)
        a_qk[g, j] = qk_kk[0:c] * decay
        low = jnp.where(strict, qk_kk[c:2 * c] * decay * spread(beta_c[rows, g * sbs:(g + 1) * sbs]), 0.0)
        t_inv[g, j] = eye - low
        pw[g, j] = (-low).astype(BF16)

    for h in range(A_HEADS // 2):
        gate_proj(h)
    for s in range(2 * A_HEADS, A_CONV_CH // LANES):
        conv_slab(s)
    ext_ref[0:CONV_PAD, :] = ext_ref[tm:tm + CONV_PAD, :]

    for rnd in range(log_c):
        if 1 <= rnd <= A_HEADS // 2:
            gate_proj(A_HEADS // 2 + rnd - 1)
        for u in units:
            bd = block_diag(pw[u], bd_p)
            if rnd == 0:
                pw[u] = jnp.dot(pw[u], bd, preferred_element_type=F32).astype(BF16)
            elif rnd == log_c - 1:
                t_inv[u] = t_inv[u] + jnp.dot(t_inv[u].astype(BF16), bd, preferred_element_type=F32)
            else:
                res = jnp.dot(jnp.concatenate([pw[u], t_inv[u].astype(BF16)], axis=0), bd,
                              preferred_element_type=F32)
                pw[u] = res[0:c].astype(BF16)
                t_inv[u] = t_inv[u] + res[c:2 * c]

    uw, a_h, q_dec, k_dec = {}, {}, {}, {}
    for h in range(A_HEADS):
        g, hp = divmod(h, sbs)
        qh = q_ref[:, h * A_DK:(h + 1) * A_DK]
        kh = k_ref[:, h * A_DK:(h + 1) * A_DK]
        vh = v_ref[:, h * A_DV:(h + 1) * A_DV]
        gc = gcum_c[:, h:h + 1]
        bc = beta_c[:, h:h + 1]
        eg = jnp.exp(gc)
        rhs = jnp.concatenate([(vh * bc).astype(BF16), (kh * (bc * eg)).astype(BF16)], axis=1)
        q_dec[h] = (qh * eg).astype(BF16)
        k_dec[h] = (kh * jnp.exp(gtot_c[:, h:h + 1] - gc)).astype(BF16)
        for j in range(n_chunks):
            rows = slice(j * c, (j + 1) * c)
            t_hc = t_inv[g, j][:, hp * c:(hp + 1) * c].astype(BF16)
            uw[h, j] = jnp.dot(t_hc, rhs[rows], preferred_element_type=F32)
            a_h[h, j] = a_qk[g, j][:, hp * c:(hp + 1) * c].astype(BF16)

    st = {h: state_ref[h] for h in range(A_HEADS)}
    o_parts = {h: [] for h in range(A_HEADS)}
    for j in range(n_chunks):
        rows = slice(j * c, (j + 1) * c)
        wq = {}
        for h in range(A_HEADS):
            w = uw[h, j][:, A_DV:].astype(BF16)
            wq[h] = jnp.dot(jnp.concatenate([w, q_dec[h][rows]], axis=0), st[h].astype(BF16),
                            preferred_element_type=F32)
        for h in range(A_HEADS):
            v_new = (uw[h, j][:, 0:A_DV] - wq[h][0:c]).astype(BF16)
            o_parts[h].append(wq[h][c:2 * c] + jnp.dot(a_h[h, j], v_new, preferred_element_type=F32))
            g_last = jnp.exp(gtot_c[j * c:j * c + 1, h:h + 1])
            st[h] = st[h] * g_last + lax.dot_general(k_dec[h][rows], v_new, _TN,
                                                     preferred_element_type=F32)

    for h in range(A_HEADS):
        state_ref[h] = st[h]
        o = _rms_rows(jnp.concatenate(o_parts[h], axis=0), ogain_ref[...])
        og_ref[:, h * A_DV:(h + 1) * A_DV] = (o * gate_ref[:, h * A_DV:(h + 1) * A_DV]).astype(BF16)

    out_ref[0] = x + jnp.dot(og_ref[...], wout_ref[...], preferred_element_type=F32)


def _deltanet_layer(x, norm, w_in, conv_w, a_log, dt_bias, o_gain, w_out):
    b, t, d = x.shape
    tm = A_TILE
    w_qkv = w_in[:, :A_CONV_CH].astype(BF16)
    w_z = w_in[:, A_CONV_CH:A_CONV_CH + A_V_W].astype(BF16)
    w_ab = w_in[:, A_CONV_CH + A_V_W:].astype(BF16)
    in_specs = [
        pl.BlockSpec((1, tm, d), lambda bi, ti: (bi, ti, 0)),
        _const_spec((1, d)),
        _const_spec((d, A_CONV_CH)),
        _const_spec((d, A_V_W)),
        _const_spec((d, 2 * A_HEADS)),
        _const_spec((2 * A_HEADS, d)),
        _const_spec((CONV_W, A_CONV_CH)),
        _const_spec((1, A_HEADS)),
        _const_spec((1, A_HEADS)),
        _const_spec((A_HEADS, 1)),
        _const_spec((A_HEADS, 1)),
        _const_spec((1, A_DV)),
        _const_spec((A_V_W, d)),
    ]
    scratch = [
        pltpu.VMEM((tm + CONV_PAD, A_CONV_CH), F32),
        pltpu.VMEM((A_HEADS, A_DK, A_DV), F32),
        pltpu.VMEM((tm, A_QK_W), F32),
        pltpu.VMEM((tm, A_QK_W), F32),
        pltpu.VMEM((tm, A_V_W), F32),
        pltpu.VMEM((tm, A_V_W), F32),
        pltpu.VMEM((tm, A_V_W), BF16),
    ]
    return pl.pallas_call(
        _deltanet_kernel,
        out_shape=jax.ShapeDtypeStruct((b, t, d), F32),
        grid=(b, t // tm),
        in_specs=in_specs,
        out_specs=pl.BlockSpec((1, tm, d), lambda bi, ti: (bi, ti, 0)),
        scratch_shapes=scratch,
        compiler_params=pltpu.CompilerParams(
            dimension_semantics=("arbitrary", "arbitrary"), vmem_limit_bytes=VMEM_LIMIT),
        name="deltanet_layer",
    )(x, norm.reshape(1, d), w_qkv, w_z, w_ab, w_ab.T, conv_w,
      a_log.reshape(1, A_HEADS), dt_bias.reshape(1, A_HEADS),
      a_log.reshape(A_HEADS, 1), dt_bias.reshape(A_HEADS, 1),
      o_gain.reshape(1, A_DV), w_out.astype(BF16))


def _head_sum_matrix(width):
    r = lax.broadcasted_iota(jnp.int32, (width, width), 0) // B_HD
    c = lax.broadcasted_iota(jnp.int32, (width, width), 1) // B_HD
    return (r == c).astype(BF16)


def _head_rms(x, gain_tiled):
    width = x.shape[-1]
    summer = _head_sum_matrix(LANES)
    sq = x * x
    hi = sq.astype(BF16)
    lo = (sq - hi.astype(F32)).astype(BF16)
    parts = []
    for s in range(width // LANES):
        cols = slice(s * LANES, (s + 1) * LANES)
        parts.append(jnp.dot(hi[:, cols], summer, preferred_element_type=F32)
                     + jnp.dot(lo[:, cols], summer, preferred_element_type=F32))
    ss = jnp.concatenate(parts, axis=-1) if len(parts) > 1 else parts[0]
    return x * lax.rsqrt(ss * (1.0 / B_HD) + EPS) * gain_tiled


def _shared_kv_kernel(x_ref, norm_ref, wkv_ref, kgain_ref, k_ref, v_ref):
    hb = _rms_rows(x_ref[0], norm_ref[...]).astype(BF16)
    kv = jnp.dot(hb, wkv_ref[...], preferred_element_type=F32)
    k_ref[0] = _head_rms(kv[:, :B_KV_W], kgain_ref[...])
    v_ref[0] = kv[:, B_KV_W:]


def _shared_kv(x, kv_norm, w_kv, k_gain):
    b, t, d = x.shape
    tk = min(KV_TILE, t)
    out = jax.ShapeDtypeStruct((b, t, B_KV_W), F32)
    spec_out = pl.BlockSpec((1, tk, B_KV_W), lambda bi, ti: (bi, ti, 0))
    return pl.pallas_call(
        _shared_kv_kernel,
        out_shape=(out, out),
        grid=(b, t // tk),
        in_specs=[pl.BlockSpec((1, tk, d), lambda bi, ti: (bi, ti, 0)),
                  _const_spec((1, d)), _const_spec((d, 2 * B_KV_W)), _const_spec((1, B_KV_W))],
        out_specs=(spec_out, spec_out),
        compiler_params=pltpu.CompilerParams(
            dimension_semantics=("arbitrary", "arbitrary"), vmem_limit_bytes=VMEM_LIMIT),
        name="shared_kv",
    )(x, kv_norm.reshape(1, d), w_kv.astype(BF16), jnp.tile(k_gain, B_KV_HEADS).reshape(1, B_KV_W))


def _bucket_ranges():
    dist = np.arange(WINDOW)
    max_exact = N_BUCKETS // 2
    large = max_exact + (np.log(np.maximum(dist, 1) / max_exact) / np.log(MAX_DIST / max_exact)
                         * (N_BUCKETS - max_exact)).astype(np.int64)
    large = np.minimum(large, N_BUCKETS - 1)
    bucket = np.where(dist < max_exact, dist, large)
    ranges = []
    for bkt in range(N_BUCKETS):
        idx = np.nonzero(bucket == bkt)[0]
        if idx.size:
            assert np.all(np.diff(idx) == 1)
            ranges.append((bkt, int(idx[0]), int(idx[-1])))
    return ranges


def _band_bias_kernel(rel_ref, out_ref):
    h = pl.program_id(0)
    qi = lax.broadcasted_iota(jnp.int32, (BLOCK, 2 * BLOCK), 0)
    si = lax.broadcasted_iota(jnp.int32, (BLOCK, 2 * BLOCK), 1)
    dist = qi + BLOCK - si
    acc = jnp.full((BLOCK, 2 * BLOCK), NEG, F32)
    for bkt, lo, hi in _bucket_ranges():
        acc = jnp.where((dist >= lo) & (dist <= hi), rel_ref[bkt, h], acc)
    out_ref[0] = acc


def _band_bias(rel_bias):
    return pl.pallas_call(
        _band_bias_kernel,
        out_shape=jax.ShapeDtypeStruct((B_Q_HEADS, BLOCK, 2 * BLOCK), F32),
        grid=(B_Q_HEADS,),
        in_specs=[pl.BlockSpec(memory_space=pltpu.SMEM)],
        out_specs=pl.BlockSpec((1, BLOCK, 2 * BLOCK), lambda h: (h, 0, 0)),
        compiler_params=pltpu.CompilerParams(dimension_semantics=("arbitrary",)),
        name="band_bias",
    )(rel_bias)


def _swa_kernel(sinks_ref, x_ref, norm_ref, wq_ref, wz_ref, qgain_ref,
                kprev_ref, kcur_ref, vprev_ref, vcur_ref, bias_ref, wout_ref,
                out_ref,
                q_ref, ka_ref, kb_ref, va_ref, vb_ref, o_ref):
    tb = B_TILE
    first = pl.program_id(1) == 0
    x = x_ref[0]
    hb = _rms_rows(x, norm_ref[...]).astype(BF16)
    q = jnp.dot(hb, wq_ref[...], preferred_element_type=F32)
    q_ref[...] = _head_rms(q, qgain_ref[...] * (B_HD ** -0.5)).astype(BF16)

    lane = lax.broadcasted_iota(jnp.int32, (BLOCK + tb, LANES), 1)
    low_half = lane < B_HD
    for t2 in range(B_KV_W // LANES):
        cols = slice(t2 * LANES, (t2 + 1) * LANES)
        for src_prev, src_cur, dst_a, dst_b in ((kprev_ref, kcur_ref, ka_ref, kb_ref),
                                                (vprev_ref, vcur_ref, va_ref, vb_ref)):
            full = jnp.concatenate([src_prev[0, :, cols], src_cur[0, :, cols]], axis=0)
            swapped = pltpu.roll(full, B_HD, 1)
            zero = jnp.zeros_like(full)
            dst_a[2 * t2] = jnp.where(low_half, full, zero).astype(BF16)
            dst_b[2 * t2] = jnp.where(low_half, zero, swapped).astype(BF16)
            dst_a[2 * t2 + 1] = jnp.where(low_half, swapped, zero).astype(BF16)
            dst_b[2 * t2 + 1] = jnp.where(low_half, zero, full).astype(BF16)

    lane_q = lax.broadcasted_iota(jnp.int32, (BLOCK, LANES), 1)
    key_col = lax.broadcasted_iota(jnp.int32, (BLOCK, 2 * BLOCK), 1)
    for qb in range(tb // BLOCK):
        rows = slice(qb * BLOCK, (qb + 1) * BLOCK)
        band = slice(qb * BLOCK, qb * BLOCK + 2 * BLOCK)
        if qb == 0:
            pen = jnp.where(first & (key_col < BLOCK), NEG, 0.0)
        for h0 in range(0, B_Q_HEADS, SWA_HEAD_GROUP):
            heads = range(h0, h0 + SWA_HEAD_GROUP)
            j = h0 // B_GROUP
            kz = (ka_ref[j, band, :], kb_ref[j, band, :])
            vz = (va_ref[j, band, :], vb_ref[j, band, :])
            logits, m, pexp, inv = {}, {}, {}, {}
            for hq in heads:
                qpair = q_ref[rows, (hq // 2) * LANES:(hq // 2 + 1) * LANES]
                lg = lax.dot_general(qpair, kz[hq % 2], _NT, preferred_element_type=F32) + bias_ref[hq]
                logits[hq] = lg + pen if qb == 0 else lg
            for hq in heads:
                m[hq] = jnp.maximum(jnp.max(logits[hq], axis=-1, keepdims=True), sinks_ref[hq])
            for hq in heads:
                pe = jnp.exp(logits[hq] - m[hq])
                denom = jnp.sum(pe, axis=-1, keepdims=True) + jnp.exp(sinks_ref[hq] - m[hq])
                inv[hq] = 1.0 / denom
                pexp[hq] = pe.astype(BF16)
            for hq in heads[::2]:
                pv = (jnp.dot(pexp[hq], vz[0], preferred_element_type=F32)
                      + jnp.dot(pexp[hq + 1], vz[1], preferred_element_type=F32))
                p = hq // 2
                o_ref[rows, p * LANES:(p + 1) * LANES] = pv * jnp.where(lane_q < B_HD, inv[hq], inv[hq + 1])

    z = jnp.dot(hb, wz_ref[...], preferred_element_type=F32)
    og = (o_ref[...] * _silu(z)).astype(BF16)
    out_ref[0] = x + jnp.dot(og, wout_ref[...], preferred_element_type=F32)


def _swa_layer(x, norm, w_in, q_gain, sinks, w_out, k, v, bias):
    b, t, d = x.shape
    tb = B_TILE
    nblk = tb // BLOCK
    tile = lambda bi, ti: (bi, ti, 0)
    prev = lambda bi, ti: (bi, jnp.maximum(ti * nblk - 1, 0), 0)
    in_specs = [
        pl.BlockSpec(memory_space=pltpu.SMEM),
        pl.BlockSpec((1, tb, d), tile),
        _const_spec((1, d)),
        _const_spec((d, B_W)),
        _const_spec((d, B_W)),
        _const_spec((1, B_W)),
        pl.BlockSpec((1, BLOCK, B_KV_W), prev),
        pl.BlockSpec((1, tb, B_KV_W), tile),
        pl.BlockSpec((1, BLOCK, B_KV_W), prev),
        pl.BlockSpec((1, tb, B_KV_W), tile),
        _const_spec((B_Q_HEADS, BLOCK, 2 * BLOCK)),
        _const_spec((B_W, d)),
    ]
    kv_scratch = pltpu.VMEM((B_KV_HEADS, BLOCK + tb, LANES), BF16)
    scratch = [pltpu.VMEM((tb, B_W), BF16), kv_scratch, kv_scratch, kv_scratch, kv_scratch,
               pltpu.VMEM((tb, B_W), F32)]
    return pl.pallas_call(
        _swa_kernel,
        out_shape=jax.ShapeDtypeStruct((b, t, d), F32),
        grid=(b, t // tb),
        in_specs=in_specs,
        out_specs=pl.BlockSpec((1, tb, d), tile),
        scratch_shapes=scratch,
        compiler_params=pltpu.CompilerParams(
            dimension_semantics=("arbitrary", "arbitrary"), vmem_limit_bytes=VMEM_LIMIT),
        name="swa_layer",
    )(sinks, x, norm.reshape(1, d), w_in[:, :B_W].astype(BF16), w_in[:, B_W:].astype(BF16),
      jnp.tile(q_gain, B_Q_HEADS).reshape(1, B_W), k, k, v, v, bias, w_out.astype(BF16))


def kernel(x, a_norm, a_w_in, a_conv, a_A_log, a_dt_bias, a_o_gain, a_w_out, kv_norm, w_kv, k_gain, rel_bias, b_norm, b_w_in, b_q_gain, b_sinks, b_w_out):
    n_a = a_w_in.shape[0]
    n_b = b_w_in.shape[0]
    for i in range(n_a):
        x = _deltanet_layer(x, a_norm[i], a_w_in[i], a_conv[i], a_A_log[i], a_dt_bias[i],
                            a_o_gain[i], a_w_out[i])
    k, v = _shared_kv(x, kv_norm, w_kv, k_gain)
    bias = _band_bias(rel_bias)
    for j in range(n_b):
        x = _swa_layer(x, b_norm[j], b_w_in[j], b_q_gain[j], b_sinks[j], b_w_out[j], k, v, bias)
    return x
```

```python
import functools
import math

import numpy as np
import jax
import jax.numpy as jnp
from jax import lax
from jax.experimental import pallas as pl
from jax.experimental.pallas import tpu as pltpu

F32 = jnp.float32
BF16 = jnp.bfloat16

D_MODEL = 1024
EPS = 1e-6

A_HEADS = 8
A_DK = 128
A_DV = 256
A_QK_W = A_HEADS * A_DK
A_V_W = A_HEADS * A_DV
A_CONV_CH = 2 * A_QK_W + A_V_W
CONV_W = 4
CHUNK = 64
A_TILE = 256
CONV_PAD = 8
SBS_HEADS = 4

B_Q_HEADS = 32
B_KV_HEADS = 4
B_GROUP = B_Q_HEADS // B_KV_HEADS
B_HD = 64
B_W = B_Q_HEADS * B_HD
B_KV_W = B_KV_HEADS * B_HD
WINDOW = 128
BLOCK = 128
B_TILE = 256
SWA_HEAD_GROUP = 8
KV_TILE = 512
N_BUCKETS = 32
MAX_DIST = 128
NEG = -1e30
LOG2E = math.log2(math.e)
GATE_SLAB = 256

LANES = 128
VMEM_LIMIT = 56 * 1024 * 1024

_NT = (((1,), (1,)), ((), ()))
_NN = (((1,), (0,)), ((), ()))
_TN = (((0,), (0,)), ((), ()))


def _mm(a, b, dims=_NN):
    return lax.dot_general(a.astype(BF16), b.astype(BF16), dims, preferred_element_type=F32)


def _mm32(a, b, dims=_NN):
    return lax.dot_general(a, b, dims, precision=lax.Precision.HIGHEST, preferred_element_type=F32)


def _silu(x):
    return x * jax.nn.sigmoid(x)


def _rms_rows(x, gain):
    return x * lax.rsqrt(jnp.mean(x * x, axis=-1, keepdims=True) + EPS) * gain


def _const_spec(shape):
    nd = len(shape)
    return pl.BlockSpec(shape, lambda *_: (0,) * nd, pipeline_mode=pl.Buffered(1))


def _deltanet_kernel(x_ref, norm_ref, wqkv_ref, wz_ref, wab_ref, wabt_ref, conv_ref,
                     alog_ref, dtb_ref, alog_c_ref, dtb_c_ref, ogain_ref, wout_ref,
                     out_ref,
                     ext_ref, state_ref, q_ref, k_ref, v_ref, gate_ref, og_ref):
    tm = A_TILE
    c = CHUNK
    sbs = SBS_HEADS
    n_chunks = tm // c
    log_c = int(math.log2(c))
    assert n_chunks == sbs
    first = pl.program_id(1) == 0

    @pl.when(first)
    def _():
        ext_ref[0:CONV_PAD, :] = jnp.zeros((CONV_PAD, A_CONV_CH), F32)
        state_ref[...] = jnp.zeros_like(state_ref)

    x = x_ref[0]
    hb = _rms_rows(x, norm_ref[...]).astype(BF16)


    def project(lo, hi):
        ext_ref[CONV_PAD:CONV_PAD + tm, lo:hi] = _mm(hb, wqkv_ref[:, lo:hi])

    def conv_slab(s):
        cols = slice(s * LANES, (s + 1) * LANES)
        ext = ext_ref[:, cols]
        acc = ext * conv_ref[0:1, cols]
        for j in range(1, CONV_W):
            acc = ext * conv_ref[j:j + 1, cols] + pltpu.roll(acc, 1, 0)
        y = _silu(acc[CONV_PAD:CONV_PAD + tm])
        if s < A_HEADS:
            y = y * (lax.rsqrt(jnp.sum(y * y, axis=-1, keepdims=True) + EPS) * (A_DK ** -0.5))
            q_ref[:, cols] = y
        elif s < 2 * A_HEADS:
            y = y * lax.rsqrt(jnp.sum(y * y, axis=-1, keepdims=True) + EPS)
            k_ref[:, s * LANES - A_QK_W:(s + 1) * LANES - A_QK_W] = y
        else:
            v_ref[:, s * LANES - 2 * A_QK_W:(s + 1) * LANES - 2 * A_QK_W] = y

    def gate_proj(h):
        gate_ref[:, h * A_DV:(h + 1) * A_DV] = _silu(_mm(hb, wz_ref[:, h * A_DV:(h + 1) * A_DV]))

    slab = 512
    for lo in range(0, 2 * A_QK_W, slab):
        project(lo, lo + slab)

    ab = _mm(hb, wab_ref[...])
    ab_t = lax.dot_general(wabt_ref[...], hb, _NT, preferred_element_type=F32)
    beta_c = jax.nn.sigmoid(ab[:, 0:A_HEADS])
    g_c = -jnp.exp(alog_ref[...]) * jax.nn.softplus(ab[:, A_HEADS:2 * A_HEADS] + dtb_ref[...])
    g_r = -jnp.exp(alog_c_ref[...]) * jax.nn.softplus(ab_t[A_HEADS:2 * A_HEADS, :] + dtb_c_ref[...])

    ri = lax.broadcasted_iota(jnp.int32, (tm, tm), 0)
    ci = lax.broadcasted_iota(jnp.int32, (tm, tm), 1)
    same = (ri >> log_c) == (ci >> log_c)
    gcum_c = _mm32((same & (ri >= ci)).astype(F32), g_c)
    gtot_c = _mm32(same.astype(F32), g_c)
    gcum_r = _mm32(g_r, (same & (ri <= ci)).astype(F32))
    gcum_rot = [gcum_r] + [pltpu.roll(gcum_r, k * c, 1) for k in range(1, n_chunks)]

    for s in range(2 * A_HEADS):
        conv_slab(s)
    for lo in range(2 * A_QK_W, A_CONV_CH, slab):
        project(lo, lo + slab)

    sw = sbs * c
    row_i = lax.broadcasted_iota(jnp.int32, (c, sw), 0)
    lane_i = lax.broadcasted_iota(jnp.int32, (c, sw), 1)
    pos_i = lane_i & (c - 1)
    blk_i = lane_i >> log_c
    incl = row_i >= pos_i
    strict = row_i > pos_i
    eye = (row_i == pos_i).astype(F32)
    bd_p = ((lax.broadcasted_iota(jnp.int32, (sw, sw), 0) >> log_c)
            == (lax.broadcasted_iota(jnp.int32, (sw, sw), 1) >> log_c))
    bd_k = ((lax.broadcasted_iota(jnp.int32, (sw, sbs * A_DK), 0) >> log_c)
            == (lax.broadcasted_iota(jnp.int32, (sw, sbs * A_DK), 1) >> int(math.log2(A_DK))))

    def spread(cols):
        out = cols[:, sbs - 1:sbs]
        for hp in range(sbs - 2, -1, -1):
            out = jnp.where(blk_i == hp, cols[:, hp:hp + 1], out)
        return out

    def block_diag(m, mask):
        return jnp.where(mask, jnp.concatenate([m] * sbs, axis=0), jnp.zeros((), m.dtype))

    units =[(g, j) for g in range(A_HEADS // sbs) for j in range(n_chunks)]
    a_qk, t_inv, pw = {}, {}, {}
    for g, j in units:
        rows = slice(j * c, (j + 1) * c)
        hcols = slice(g * sbs * A_DK, (g + 1) * sbs * A_DK)
        qg = q_ref[rows, hcols].astype(BF16)
        kg = k_ref[rows, hcols].astype(BF16)
        qk_kk = lax.dot_general(jnp.concatenate([qg, kg], axis=0), block_diag(kg, bd_k), _NT,
                                preferred_element_type=F32)
        gcol = spread(gcum_c[rows, g * sbs:(g + 1) * sbs])
        grow = gcum_rot[(sbs - 1 - j) % n_chunks][g * sbs + sbs - 1:g * sbs + sbs, :]
        for hp in range(sbs - 2, -1, -1):
            grow = jnp.where(blk_i[0:1] == hp,
                             gcum_rot[(hp - j) % n_chunks][g * sbs + hp:g * sbs + hp + 1, :], grow)
        decay = jnp.where(incl, jnp.exp(jnp.minimum(gcol - grow, 0.0)), 0.0)
        a_qk[g, j] = qk_kk[0:c] * decay
        low = jnp.where(strict, qk_kk[c:2 * c] * decay * spread(beta_c[rows, g * sbs:(g + 1) * sbs]), 0.0)
        t_inv[g, j] = eye - low
        pw[g, j] = (-low).astype(BF16)

    for h in range(A_HEADS // 2):
        gate_proj(h)
    for s in range(2 * A_HEADS, A_CONV_CH // LANES):
        conv_slab(s)
    ext_ref[0:CONV_PAD, :] = ext_ref[tm:tm + CONV_PAD, :]

    for rnd in range(log_c):
        if 1 <= rnd <= A_HEADS // 2:
            gate_proj(A_HEADS // 2 + rnd - 1)
        for u in units:
            bd = block_diag(pw[u], bd_p)
            if rnd == 0:
                pw[u] = jnp.dot(pw[u], bd, preferred_element_type=F32).astype(BF16)
            elif rnd == log_c - 1:
                t_inv[u] = t_inv[u] + jnp.dot(t_inv[u].astype(BF16), bd, preferred_element_type=F32)
            else:
                res = jnp.dot(jnp.concatenate([pw[u], t_inv[u].astype(BF16)], axis=0), bd,
                              preferred_element_type=F32)
                pw[u] = res[0:c].astype(BF16)
                t_inv[u] = t_inv[u] + res[c:2 * c]

    uw, a_h, q_dec, k_dec = {}, {}, {}, {}
    for h in range(A_HEADS):
        g, hp = divmod(h, sbs)
        qh = q_ref[:, h * A_DK:(h + 1) * A_DK]
        kh = k_ref[:, h * A_DK:(h + 1) * A_DK]
        vh = v_ref[:, h * A_DV:(h + 1) * A_DV]
        gc = gcum_c[:, h:h + 1]
        bc = beta_c[:, h:h + 1]
        eg = jnp.exp(gc)
        rhs = jnp.concatenate([(vh * bc).astype(BF16), (kh * (bc * eg)).astype(BF16)], axis=1)
        q_dec[h] = (qh * eg).astype(BF16)
        k_dec[h] = (kh * jnp.exp(gtot_c[:, h:h + 1] - gc)).astype(BF16)
        for j in range(n_chunks):
            rows = slice(j * c, (j + 1) * c)
            t_hc = t_inv[g, j][:, hp * c:(hp + 1) * c].astype(BF16)
            uw[h, j] = jnp.dot(t_hc, rhs[rows], preferred_element_type=F32)
            a_h[h, j] = a_qk[g, j][:, hp * c:(hp + 1) * c].astype(BF16)

    st = {h: state_ref[h] for h in range(A_HEADS)}
    o_parts = {h: [] for h in range(A_HEADS)}
    for j in range(n_chunks):
        rows = slice(j * c, (j + 1) * c)
        wq = {}
        for h in range(A_HEADS):
            w = uw[h, j][:, A_DV:].astype(BF16)
            wq[h] = jnp.dot(jnp.concatenate([w, q_dec[h][rows]], axis=0), st[h].astype(BF16),
                            preferred_element_type=F32)
        for h in range(A_HEADS):
            v_new = (uw[h, j][:, 0:A_DV] - wq[h][0:c]).astype(BF16)
            o_parts[h].append(wq[h][c:2 * c] + jnp.dot(a_h[h, j], v_new, preferred_element_type=F32))
            g_last = jnp.exp(gtot_c[j * c:j * c + 1, h:h + 1])
            st[h] = st[h] * g_last + lax.dot_general(k_dec[h][rows], v_new, _TN,
                                                     preferred_element_type=F32)

    for h in range(A_HEADS):
        state_ref[h] = st[h]
        o = _rms_rows(jnp.concatenate(o_parts[h], axis=0), ogain_ref[...])
        og_ref[:, h * A_DV:(h + 1) * A_DV] = (o * gate_ref[:, h * A_DV:(h + 1) * A_DV]).astype(BF16)

    out_ref[0] = x + jnp.dot(og_ref[...], wout_ref[...], preferred_element_type=F32)


def _deltanet_layer(x, norm, w_in, conv_w, a_log, dt_bias, o_gain, w_out):
    b, t, d = x.shape
    tm = A_TILE
    w_qkv = w_in[:, :A_CONV_CH].astype(BF16)
    w_z = w_in[:, A_CONV_CH:A_CONV_CH + A_V_W].astype(BF16)
    w_ab = w_in[:, A_CONV_CH + A_V_W:].astype(BF16)
    in_specs = [
        pl.BlockSpec((1, tm, d), lambda bi, ti: (bi, ti, 0)),
        _const_spec((1, d)),
        _const_spec((d, A_CONV_CH)),
        _const_spec((d, A_V_W)),
        _const_spec((d, 2 * A_HEADS)),
        _const_spec((2 * A_HEADS, d)),
        _const_spec((CONV_W, A_CONV_CH)),
        _const_spec((1, A_HEADS)),
        _const_spec((1, A_HEADS)),
        _const_spec((A_HEADS, 1)),
        _const_spec((A_HEADS, 1)),
        _const_spec((1, A_DV)),
        _const_spec((A_V_W, d)),
    ]
    scratch = [
        pltpu.VMEM((tm + CONV_PAD, A_CONV_CH), F32),
        pltpu.VMEM((A_HEADS, A_DK, A_DV), F32),
        pltpu.VMEM((tm, A_QK_W), F32),
        pltpu.VMEM((tm, A_QK_W), F32),
        pltpu.VMEM((tm, A_V_W), F32),
        pltpu.VMEM((tm, A_V_W), F32),
        pltpu.VMEM((tm, A_V_W), BF16),
    ]
    return pl.pallas_call(
        _deltanet_kernel,
        out_shape=jax.ShapeDtypeStruct((b, t, d), F32),
        grid=(b, t // tm),
        in_specs=in_specs,
        out_specs=pl.BlockSpec((1, tm, d), lambda bi, ti: (bi, ti, 0)),
        scratch_shapes=scratch,
        compiler_params=pltpu.CompilerParams(
            dimension_semantics=("arbitrary", "arbitrary"), vmem_limit_bytes=VMEM_LIMIT),
        name="deltanet_layer",
    )(x, norm.reshape(1, d), w_qkv, w_z, w_ab, w_ab.T, conv_w,
      a_log.reshape(1, A_HEADS), dt_bias.reshape(1, A_HEADS),
      a_log.reshape(A_HEADS, 1), dt_bias.reshape(A_HEADS, 1),
      o_gain.reshape(1, A_DV), w_out.astype(BF16))


def _head_sum_matrix(width):
    r = lax.broadcasted_iota(jnp.int32, (width, width), 0) // B_HD
    c = lax.broadcasted_iota(jnp.int32, (width, width), 1) // B_HD
    return (r == c).astype(BF16)


def _head_rms(x, gain_tiled):
    width = x.shape[-1]
    summer = _head_sum_matrix(LANES)
    sq = x * x
    hi = sq.astype(BF16)
    lo = (sq - hi.astype(F32)).astype(BF16)
    parts = []
    for s in range(width // LANES):
        cols = slice(s * LANES, (s + 1) * LANES)
        parts.append(jnp.dot(hi[:, cols], summer, preferred_element_type=F32)
                     + jnp.dot(lo[:, cols], summer, preferred_element_type=F32))
    ss = jnp.concatenate(parts, axis=-1) if len(parts) > 1 else parts[0]
    return x * lax.rsqrt(ss * (1.0 / B_HD) + EPS) * gain_tiled


def _shared_kv_kernel(x_ref, norm_ref, wkv_ref, kgain_ref, k_ref, v_ref):
    hb = _rms_rows(x_ref[0], norm_ref[...]).astype(BF16)
    kv = jnp.dot(hb, wkv_ref[...], preferred_element_type=F32)
    k_ref[0] = _head_rms(kv[:, :B_KV_W], kgain_ref[...])
    v_ref[0] = kv[:, B_KV_W:]


def _shared_kv(x, kv_norm, w_kv, k_gain):
    b, t, d = x.shape
    tk = min(KV_TILE, t)
    out = jax.ShapeDtypeStruct((b, t, B_KV_W), F32)
    spec_out = pl.BlockSpec((1, tk, B_KV_W), lambda bi, ti: (bi, ti, 0))
    return pl.pallas_call(
        _shared_kv_kernel,
        out_shape=(out, out),
        grid=(b, t // tk),
        in_specs=[pl.BlockSpec((1, tk, d), lambda bi, ti: (bi, ti, 0)),
                  _const_spec((1, d)), _const_spec((d, 2 * B_KV_W)), _const_spec((1, B_KV_W))],
        out_specs=(spec_out, spec_out),
        compiler_params=pltpu.CompilerParams(
            dimension_semantics=("arbitrary", "arbitrary"), vmem_limit_bytes=VMEM_LIMIT),
        name="shared_kv",
    )(x, kv_norm.reshape(1, d), w_kv.astype(BF16), jnp.tile(k_gain, B_KV_HEADS).reshape(1, B_KV_W))


def _bucket_ranges():
    dist = np.arange(WINDOW)
    max_exact = N_BUCKETS // 2
    large = max_exact + (np.log(np.maximum(dist, 1) / max_exact) / np.log(MAX_DIST / max_exact)
                         * (N_BUCKETS - max_exact)).astype(np.int64)
    large = np.minimum(large, N_BUCKETS - 1)
    bucket = np.where(dist < max_exact, dist, large)
    ranges = []
    for bkt in range(N_BUCKETS):
        idx = np.nonzero(bucket == bkt)[0]
        if idx.size:
            assert np.all(np.diff(idx) == 1)
            ranges.append((bkt, int(idx[0]), int(idx[-1])))
    return ranges


def _band_bias_kernel(rel_ref, out_ref):
    h = pl.program_id(0)
    qi = lax.broadcasted_iota(jnp.int32, (BLOCK, BLOCK), 0)
    si = lax.broadcasted_iota(jnp.int32, (BLOCK, BLOCK), 1)
    dist = jnp.where(si > qi, qi - si + BLOCK, qi - si)
    ranges = _bucket_ranges()
    assert ranges[0][1] == 0 and ranges[-1][2] == WINDOW - 1
    acc = jnp.zeros((BLOCK, BLOCK), F32)
    for bkt, lo, hi in ranges:
        acc = jnp.where((dist >= lo) & (dist <= hi), rel_ref[bkt, h] * LOG2E, acc)
    out_ref[0] = acc


def _band_bias(rel_bias):
    return pl.pallas_call(
        _band_bias_kernel,
        out_shape=jax.ShapeDtypeStruct((B_Q_HEADS, BLOCK, BLOCK), F32),
        grid=(B_Q_HEADS,),
        in_specs=[pl.BlockSpec(memory_space=pltpu.SMEM)],
        out_specs=pl.BlockSpec((1, BLOCK, BLOCK), lambda h: (h, 0, 0)),
        compiler_params=pltpu.CompilerParams(dimension_semantics=("arbitrary",)),
        name="band_bias",
    )(rel_bias)


def _swa_kernel(sinks_ref, x_ref, norm_ref, wq_ref, wz_ref, qgain_ref,
                kprev_ref, kcur_ref, vprev_ref, vcur_ref, bias_ref, wout_ref,
                out_ref,
                q_ref, ka_ref, kb_ref, va_ref, vb_ref, o_ref, gate_ref):
    tb = B_TILE
    first = pl.program_id(1) == 0
    x = x_ref[0]
    hb = _rms_rows(x, norm_ref[...]).astype(BF16)
    q = jnp.dot(hb, wq_ref[...], preferred_element_type=F32)
    q_ref[...] = _head_rms(q, qgain_ref[...] * (B_HD ** -0.5 * LOG2E)).astype(BF16)

    def gate_proj(p):
        cols = slice(p * GATE_SLAB, (p + 1) * GATE_SLAB)
        gate_ref[:, cols] = _silu(jnp.dot(hb, wz_ref[:, cols], preferred_element_type=F32))

    lane = lax.broadcasted_iota(jnp.int32, (BLOCK + tb, LANES), 1)
    low_half = lane < B_HD
    for t2 in range(B_KV_W // LANES):
        cols = slice(t2 * LANES, (t2 + 1) * LANES)
        for src_prev, src_cur, dst_a, dst_b in ((kprev_ref, kcur_ref, ka_ref, kb_ref),
                                                (vprev_ref, vcur_ref, va_ref, vb_ref)):
            full = jnp.concatenate([src_prev[0, :, cols], src_cur[0, :, cols]], axis=0)
            swapped = pltpu.roll(full, B_HD, 1)
            zero = jnp.zeros_like(full)
            dst_a[2 * t2] = jnp.where(low_half, full, zero).astype(BF16)
            dst_b[2 * t2] = jnp.where(low_half, zero, swapped).astype(BF16)
            dst_a[2 * t2 + 1] = jnp.where(low_half, swapped, zero).astype(BF16)
            dst_b[2 * t2 + 1] = jnp.where(low_half, zero, full).astype(BF16)

    lane_q = lax.broadcasted_iota(jnp.int32, (BLOCK, LANES), 1)
    from_prev = (lax.broadcasted_iota(jnp.int32, (BLOCK, BLOCK), 1)
                 > lax.broadcasted_iota(jnp.int32, (BLOCK, BLOCK), 0))
    pen = jnp.where(first & from_prev, NEG, 0.0)

    groups = [(qb, h0) for qb in range(tb // BLOCK) for h0 in range(0, B_Q_HEADS, SWA_HEAD_GROUP)]

    def qk_logits(qb, h0):
        rows = slice(qb * BLOCK, (qb + 1) * BLOCK)
        band = slice(qb * BLOCK, qb * BLOCK + 2 * BLOCK)
        j = h0 // B_GROUP
        kz = (ka_ref[j, band, :], kb_ref[j, band, :])
        out = {}
        for hq in range(h0, h0 + SWA_HEAD_GROUP):
            qpair = q_ref[rows, (hq // 2) * LANES:(hq // 2 + 1) * LANES]
            lg2 = lax.dot_general(qpair, kz[hq % 2], _NT, preferred_element_type=F32)
            lg = jnp.where(from_prev, lg2[:, 0:BLOCK], lg2[:, BLOCK:2 * BLOCK]) + bias_ref[hq]
            out[hq] = lg + pen if qb == 0 else lg
        return out

    def softmax_pv(qb, h0, logits):
        rows = slice(qb * BLOCK, (qb + 1) * BLOCK)
        band = slice(qb * BLOCK, qb * BLOCK + 2 * BLOCK)
        j = h0 // B_GROUP
        vz = (va_ref[j, band, :], vb_ref[j, band, :])
        heads = range(h0, h0 + SWA_HEAD_GROUP)
        m, pexp, inv = {}, {}, {}
        for hq in heads:
            m[hq] = jnp.maximum(jnp.max(logits[hq], axis=-1, keepdims=True), sinks_ref[hq] * LOG2E)
        for hq in heads:
            pe = jnp.exp2(logits[hq] - m[hq])
            denom = jnp.sum(pe, axis=-1, keepdims=True) + jnp.exp2(sinks_ref[hq] * LOG2E - m[hq])
            inv[hq] = 1.0 / denom
            pexp[hq] = jnp.concatenate([jnp.where(from_prev, pe, 0.0).astype(BF16),
                                        jnp.where(from_prev, 0.0, pe).astype(BF16)], axis=1)
        for hq in heads[::2]:
            pv = (jnp.dot(pexp[hq], vz[0], preferred_element_type=F32)
                  + jnp.dot(pexp[hq + 1], vz[1], preferred_element_type=F32))
            p = hq // 2
            o_ref[rows, p * LANES:(p + 1) * LANES] = pv * jnp.where(lane_q < B_HD, inv[hq], inv[hq + 1])

    n_gate = B_W // GATE_SLAB
    logits = qk_logits(*groups[0])
    for gi, grp in enumerate(groups):
        nxt = qk_logits(*groups[gi + 1]) if gi + 1 < len(groups) else None
        softmax_pv(*grp, logits)
        for p in range(gi * n_gate // len(groups), (gi + 1) * n_gate // len(groups)):
            gate_proj(p)
        logits = nxt

    og = (o_ref[...] * gate_ref[...]).astype(BF16)
    out_ref[0] = x + jnp.dot(og, wout_ref[...], preferred_element_type=F32)


def _swa_layer(x, norm, w_in, q_gain, sinks, w_out, k, v, bias):
    b, t, d = x.shape
    tb = B_TILE
    nblk = tb // BLOCK
    tile = lambda bi, ti: (bi, ti, 0)
    prev = lambda bi, ti: (bi, jnp.maximum(ti * nblk - 1, 0), 0)
    in_specs = [
        pl.BlockSpec(memory_space=pltpu.SMEM),
        pl.BlockSpec((1, tb, d), tile),
        _const_spec((1, d)),
        _const_spec((d, B_W)),
        _const_spec((d, B_W)),
        _const_spec((1, B_W)),
        pl.BlockSpec((1, BLOCK, B_KV_W), prev),
        pl.BlockSpec((1, tb, B_KV_W), tile),
        pl.BlockSpec((1, BLOCK, B_KV_W), prev),
        pl.BlockSpec((1, tb, B_KV_W), tile),
        _const_spec((B_Q_HEADS, BLOCK, BLOCK)),
        _const_spec((B_W, d)),
    ]
    kv_scratch = pltpu.VMEM((B_KV_HEADS, BLOCK + tb, LANES), BF16)
    scratch = [pltpu.VMEM((tb, B_W), BF16), kv_scratch, kv_scratch, kv_scratch, kv_scratch,
               pltpu.VMEM((tb, B_W), F32), pltpu.VMEM((tb, B_W), F32)]
    return pl.pallas_call(
        _swa_kernel,
        out_shape=jax.ShapeDtypeStruct((b, t, d), F32),
        grid=(b, t // tb),
        in_specs=in_specs,
        out_specs=pl.BlockSpec((1, tb, d), tile),
        scratch_shapes=scratch,
        compiler_params=pltpu.CompilerParams(
            dimension_semantics=("arbitrary", "arbitrary"), vmem_limit_bytes=VMEM_LIMIT),
        name="swa_layer",
    )(sinks, x, norm.reshape(1, d), w_in[:, :B_W].astype(BF16), w_in[:, B_W:].astype(BF16),
      jnp.tile(q_gain, B_Q_HEADS).reshape(1, B_W), k, k, v, v, bias, w_out.astype(BF16))


def kernel(x, a_norm, a_w_in, a_conv, a_A_log, a_dt_bias, a_o_gain, a_w_out, kv_norm, w_kv, k_gain, rel_bias, b_norm, b_w_in, b_q_gain, b_sinks, b_w_out):
    n_a = a_w_in.shape[0]
    n_b = b_w_in.shape[0]
    for i in range(n_a):
        x = _deltanet_layer(x, a_norm[i], a_w_in[i], a_conv[i], a_A_log[i], a_dt_bias[i],
                            a_o_gain[i], a_w_out[i])
    k, v = _shared_kv(x, kv_norm, w_kv, k_gain)
    bias = _band_bias(rel_bias)
    for j in range(n_b):
        x = _swa_layer(x, b_norm[j], b_w_in[j], b_q_gain[j], b_sinks[j], b_w_out[j], k, v, bias)
    return x
```

```python
import functools
import math

import numpy as np
import jax
import jax.numpy as jnp
from jax import lax
from jax.experimental import pallas as pl
from jax.experimental.pallas import tpu as pltpu

F32 = jnp.float32
BF16 = jnp.bfloat16

D_MODEL = 1024
EPS = 1e-6

A_HEADS = 8
A_DK = 128
A_DV = 256
A_QK_W = A_HEADS * A_DK
A_V_W = A_HEADS * A_DV
A_CONV_CH = 2 * A_QK_W + A_V_W
A_IN_W = A_CONV_CH + A_V_W + 2 * A_HEADS
CONV_W = 4
CHUNK = 64
A_TILE = 256
CONV_PAD = 8
PROJ_SLAB = 512
SBS_HEADS = 4

B_Q_HEADS = 32
B_KV_HEADS = 4
B_GROUP = B_Q_HEADS // B_KV_HEADS
B_HD = 64
B_W = B_Q_HEADS * B_HD
B_KV_W = B_KV_HEADS * B_HD
WINDOW = 128
BLOCK = 128
B_TILE = 256
SWA_HEAD_GROUP = 8
KV_TILE = 512
N_BUCKETS = 32
MAX_DIST = 128
NEG = -1e30
LOG2E = math.log2(math.e)
GATE_SLAB = 256

LANES = 128
VMEM_LIMIT = 56 * 1024 * 1024

_NT = (((1,), (1,)), ((), ()))
_NN = (((1,), (0,)), ((), ()))
_TN = (((0,), (0,)), ((), ()))


def _mm(a, b, dims=_NN):
    return lax.dot_general(a.astype(BF16), b.astype(BF16), dims, preferred_element_type=F32)


def _split3(a):
    hi = a.astype(BF16)
    rest = a - hi.astype(F32)
    mid = rest.astype(BF16)
    lo = (rest - mid.astype(F32)).astype(BF16)
    return hi, mid, lo


def _silu(x):
    return x * jax.nn.sigmoid(x)


def _rms_rows(x, gain):
    return x * lax.rsqrt(jnp.mean(x * x, axis=-1, keepdims=True) + EPS) * gain


def _const_spec(shape):
    nd = len(shape)
    return pl.BlockSpec(shape, lambda *_: (0,) * nd, pipeline_mode=pl.Buffered(1))


def _layer_spec(shape, layer):
    nd = len(shape)
    return pl.BlockSpec((1,) + tuple(shape), lambda *_: (layer,) + (0,) * nd,
                        pipeline_mode=pl.Buffered(1))


def _deltanet_kernel(x_ref, norm_ref, win_ref, wabt_ref, conv_ref,
                     alog_ref, dtb_ref, alog_c_ref, dtb_c_ref, ogain_ref, wout_ref,
                     out_ref, *scratch):
    n_slabs = A_CONV_CH // PROJ_SLAB
    ext_refs = scratch[:n_slabs]
    state_ref, q_ref, k_ref, v_ref, gate_ref, og_ref = scratch[n_slabs:]
    tm = A_TILE
    c = CHUNK
    sbs = SBS_HEADS
    n_chunks = tm // c
    log_c = int(math.log2(c))
    assert n_chunks == sbs
    first = pl.program_id(1) == 0

    @pl.when(first)
    def _():
        for ext_ref in ext_refs:
            ext_ref[0:CONV_PAD, :] = jnp.zeros((CONV_PAD, PROJ_SLAB), F32)
        state_ref[...] = jnp.zeros_like(state_ref)

    x = x_ref[0]
    hb = _rms_rows(x, norm_ref[0]).astype(BF16)
    z_col = A_CONV_CH
    ab_col = A_CONV_CH + A_V_W


    def project(p):
        ext_refs[p][CONV_PAD:CONV_PAD + tm, :] = _mm(hb, win_ref[0, :, p * PROJ_SLAB:(p + 1) * PROJ_SLAB])

    def conv_slab(s):
        cols = slice(s * LANES, (s + 1) * LANES)
        p, off = divmod(s * LANES, PROJ_SLAB)
        ext = ext_refs[p][:, off:off + LANES]
        acc = ext * conv_ref[0, 0:1, cols]
        for j in range(1, CONV_W):
            acc = ext * conv_ref[0, j:j + 1, cols] + pltpu.roll(acc, 1, 0)
        y = _silu(acc[CONV_PAD:CONV_PAD + tm])
        if s < A_HEADS:
            y = y * (lax.rsqrt(jnp.sum(y * y, axis=-1, keepdims=True) + EPS) * (A_DK ** -0.5))
            q_ref[:, cols] = y
        elif s < 2 * A_HEADS:
            y = y * lax.rsqrt(jnp.sum(y * y, axis=-1, keepdims=True) + EPS)
            k_ref[:, s * LANES - A_QK_W:(s + 1) * LANES - A_QK_W] = y
        else:
            v_ref[:, s * LANES - 2 * A_QK_W:(s + 1) * LANES - 2 * A_QK_W] = y

    def gate_proj(h):
        wz = win_ref[0, :, z_col + h * A_DV:z_col + (h + 1) * A_DV]
        gate_ref[:, h * A_DV:(h + 1) * A_DV] = _silu(_mm(hb, wz))

    ab = _mm(hb, win_ref[0, :, ab_col:ab_col + 2 * A_HEADS])
    ab_t = lax.dot_general(wabt_ref[0], hb, _NT, preferred_element_type=F32)

    qk_slabs = 2 * A_QK_W // PROJ_SLAB
    for p in range(qk_slabs // 2):
        project(p)

    beta_c = jax.nn.sigmoid(ab[:, 0:A_HEADS])
    g_c = -jnp.exp(alog_ref[0]) * jax.nn.softplus(ab[:, A_HEADS:2 * A_HEADS] + dtb_ref[0])
    g_r = -jnp.exp(alog_c_ref[0]) * jax.nn.softplus(ab_t[A_HEADS:2 * A_HEADS, :] + dtb_c_ref[0])

    ri = lax.broadcasted_iota(jnp.int32, (tm, tm), 0)
    ci = lax.broadcasted_iota(jnp.int32, (tm, tm), 1)
    same = (ri >> log_c) == (ci >> log_c)
    sums_c = jnp.concatenate([(same & (ri >= ci)).astype(BF16), same.astype(BF16)], axis=0)
    sums_c = sum(jnp.dot(sums_c, part, preferred_element_type=F32) for part in _split3(g_c))
    gcum_c = sums_c[0:tm]
    gtot_c = sums_c[tm:2 * tm]
    gcum_r = jnp.dot(jnp.concatenate(_split3(g_r), axis=0), (same & (ri <= ci)).astype(BF16),
                     preferred_element_type=F32)
    gcum_r = gcum_r[0:A_HEADS] + gcum_r[A_HEADS:2 * A_HEADS] + gcum_r[2 * A_HEADS:3 * A_HEADS]
    gcum_rot = [gcum_r] + [pltpu.roll(gcum_r, k * c, 1) for k in range(1, n_chunks)]

    for p in range(qk_slabs // 2, qk_slabs):
        project(p)
    for s in range(2 * A_HEADS):
        conv_slab(s)
    for p in range(qk_slabs, n_slabs):
        project(p)
    for h in range(A_HEADS // 2):
        gate_proj(h)
    for s in range(2 * A_HEADS, A_CONV_CH // LANES):
        conv_slab(s)
    for ext_ref in ext_refs:
        ext_ref[0:CONV_PAD, :] = ext_ref[tm:tm + CONV_PAD, :]

    sw = sbs * c
    row_i = lax.broadcasted_iota(jnp.int32, (c, sw), 0)
    lane_i = lax.broadcasted_iota(jnp.int32, (c, sw), 1)
    pos_i = lane_i & (c - 1)
    blk_i = lane_i >> log_c
    incl = row_i >= pos_i
    strict = row_i > pos_i
    eye = (row_i == pos_i).astype(F32)
    bd_p = ((lax.broadcasted_iota(jnp.int32, (sw, sw), 0) >> log_c)
            == (lax.broadcasted_iota(jnp.int32, (sw, sw), 1) >> log_c))
    bd_k = ((lax.broadcasted_iota(jnp.int32, (sw, sbs * A_DK), 0) >> log_c)
            == (lax.broadcasted_iota(jnp.int32, (sw, sbs * A_DK), 1) >> int(math.log2(A_DK))))

    def spread(cols):
        out = cols[:, sbs - 1:sbs]
        for hp in range(sbs - 2, -1, -1):
            out = jnp.where(blk_i == hp, cols[:, hp:hp + 1], out)
        return out

    def block_diag(m, mask):
        return jnp.where(mask, jnp.concatenate([m] * sbs, axis=0), jnp.zeros((), m.dtype))

    units =[(g, j) for g in range(A_HEADS // sbs) for j in range(n_chunks)]
    a_qk, t_inv, pw = {}, {}, {}
    for g, j in units:
        rows = slice(j * c, (j + 1) * c)
        hcols = slice(g * sbs * A_DK, (g + 1) * sbs * A_DK)
        qg = q_ref[rows, hcols].astype(BF16)
        kg = k_ref[rows, hcols].astype(BF16)
        qk_kk = lax.dot_general(jnp.concatenate([qg, kg], axis=0), block_diag(kg, bd_k), _NT,
                                preferred_element_type=F32)
        gcol = spread(gcum_c[rows, g * sbs:(g + 1) * sbs])
        grow = gcum_rot[(sbs - 1 - j) % n_chunks][g * sbs + sbs - 1:g * sbs + sbs, :]
        for hp in range(sbs - 2, -1, -1):
            grow = jnp.where(blk_i[0:1] == hp,
                             gcum_rot[(hp - j) % n_chunks][g * sbs + hp:g * sbs + hp + 1, :], grow)
        decay = jnp.where(incl, jnp.exp(jnp.minimum(gcol - grow, 0.0)), 0.0)
        a_qk[g, j] = qk_kk[0:c] * decay
        low = jnp.where(strict, qk_kk[c:2 * c] * decay * spread(beta_c[rows, g * sbs:(g + 1) * sbs]), 0.0)
        t_inv[g, j] = eye - low
        pw[g, j] = (-low).astype(BF16)

    for rnd in range(log_c):
        if 1 <= rnd <= A_HEADS // 2:
            gate_proj(A_HEADS // 2 + rnd - 1)
        for u in units:
            bd = block_diag(pw[u], bd_p)
            if rnd == 0:
                pw[u] = jnp.dot(pw[u], bd, preferred_element_type=F32).astype(BF16)
            elif rnd == log_c - 1:
                t_inv[u] = t_inv[u] + jnp.dot(t_inv[u].astype(BF16), bd, preferred_element_type=F32)
            else:
                res = jnp.dot(jnp.concatenate([pw[u], t_inv[u].astype(BF16)], axis=0), bd,
                              preferred_element_type=F32)
                pw[u] = res[0:c].astype(BF16)
                t_inv[u] = t_inv[u] + res[c:2 * c]

    uw, a_h, q_dec, k_dec = {}, {}, {}, {}
    for h in range(A_HEADS):
        g, hp = divmod(h, sbs)
        qh = q_ref[:, h * A_DK:(h + 1) * A_DK]
        kh = k_ref[:, h * A_DK:(h + 1) * A_DK]
        vh = v_ref[:, h * A_DV:(h + 1) * A_DV]
        gc = gcum_c[:, h:h + 1]
        bc = beta_c[:, h:h + 1]
        eg = jnp.exp(gc)
        rhs = jnp.concatenate([(vh * bc).astype(BF16), (kh * (bc * eg)).astype(BF16)], axis=1)
        q_dec[h] = (qh * eg).astype(BF16)
        k_dec[h] = (kh * jnp.exp(gtot_c[:, h:h + 1] - gc)).astype(BF16)
        for j in range(n_chunks):
            rows = slice(j * c, (j + 1) * c)
            t_hc = t_inv[g, j][:, hp * c:(hp + 1) * c].astype(BF16)
            uw[h, j] = jnp.dot(t_hc, rhs[rows], preferred_element_type=F32)
            a_h[h, j] = a_qk[g, j][:, hp * c:(hp + 1) * c].astype(BF16)

    st = {h: state_ref[h] for h in range(A_HEADS)}
    o_parts = {h: [] for h in range(A_HEADS)}
    for j in range(n_chunks):
        rows = slice(j * c, (j + 1) * c)
        wq = {}
        for h in range(A_HEADS):
            w = uw[h, j][:, A_DV:].astype(BF16)
            wq[h] = jnp.dot(jnp.concatenate([w, q_dec[h][rows]], axis=0), st[h].astype(BF16),
                            preferred_element_type=F32)
        for h in range(A_HEADS):
            v_new = (uw[h, j][:, 0:A_DV] - wq[h][0:c]).astype(BF16)
            o_parts[h].append(wq[h][c:2 * c] + jnp.dot(a_h[h, j], v_new, preferred_element_type=F32))
            g_last = jnp.exp(gtot_c[j * c:j * c + 1, h:h + 1])
            st[h] = st[h] * g_last + lax.dot_general(k_dec[h][rows], v_new, _TN,
                                                     preferred_element_type=F32)

    for h in range(A_HEADS):
        state_ref[h] = st[h]
        o = _rms_rows(jnp.concatenate(o_parts[h], axis=0), ogain_ref[0])
        og_ref[:, h * A_DV:(h + 1) * A_DV] = (o * gate_ref[:, h * A_DV:(h + 1) * A_DV]).astype(BF16)

    out_ref[0] = x + jnp.dot(og_ref[...], wout_ref[0], preferred_element_type=F32)


def _deltanet_layer(x, layer, norm, w_in, w_ab_t, conv_w, a_log, dt_bias, o_gain, w_out):
    b, t, d = x.shape
    tm = A_TILE
    n = norm.shape[0]
    in_specs = [
        pl.BlockSpec((1, tm, d), lambda bi, ti: (bi, ti, 0)),
        _layer_spec((1, d), layer),
        _layer_spec((d, A_IN_W), layer),
        _layer_spec((2 * A_HEADS, d), layer),
        _layer_spec((CONV_W, A_CONV_CH), layer),
        _layer_spec((1, A_HEADS), layer),
        _layer_spec((1, A_HEADS), layer),
        _layer_spec((A_HEADS, 1), layer),
        _layer_spec((A_HEADS, 1), layer),
        _layer_spec((1, A_DV), layer),
        _layer_spec((A_V_W, d), layer),
    ]
    scratch = [pltpu.VMEM((tm + CONV_PAD, PROJ_SLAB), F32)
               for _ in range(A_CONV_CH // PROJ_SLAB)] + [
        pltpu.VMEM((A_HEADS, A_DK, A_DV), F32),
        pltpu.VMEM((tm, A_QK_W), F32),
        pltpu.VMEM((tm, A_QK_W), F32),
        pltpu.VMEM((tm, A_V_W), F32),
        pltpu.VMEM((tm, A_V_W), F32),
        pltpu.VMEM((tm, A_V_W), BF16),
    ]
    return pl.pallas_call(
        _deltanet_kernel,
        out_shape=jax.ShapeDtypeStruct((b, t, d), F32),
        grid=(b, t // tm),
        in_specs=in_specs,
        out_specs=pl.BlockSpec((1, tm, d), lambda bi, ti: (bi, ti, 0)),
        scratch_shapes=scratch,
        compiler_params=pltpu.CompilerParams(
            dimension_semantics=("arbitrary", "arbitrary"), vmem_limit_bytes=VMEM_LIMIT),
        name="deltanet_layer",
    )(x, norm.reshape(n, 1, d), w_in, w_ab_t, conv_w,
      a_log.reshape(n, 1, A_HEADS), dt_bias.reshape(n, 1, A_HEADS),
      a_log.reshape(n, A_HEADS, 1), dt_bias.reshape(n, A_HEADS, 1),
      o_gain.reshape(n, 1, A_DV), w_out)


def _head_sum_matrix(width):
    r = lax.broadcasted_iota(jnp.int32, (width, width), 0) // B_HD
    c = lax.broadcasted_iota(jnp.int32, (width, width), 1) // B_HD
    return (r == c).astype(BF16)


def _head_rms(x, gain_tiled):
    width = x.shape[-1]
    summer = _head_sum_matrix(LANES)
    sq = x * x
    hi = sq.astype(BF16)
    lo = (sq - hi.astype(F32)).astype(BF16)
    parts = []
    for s in range(width // LANES):
        cols = slice(s * LANES, (s + 1) * LANES)
        parts.append(jnp.dot(hi[:, cols], summer, preferred_element_type=F32)
                     + jnp.dot(lo[:, cols], summer, preferred_element_type=F32))
    ss = jnp.concatenate(parts, axis=-1) if len(parts) > 1 else parts[0]
    return x * lax.rsqrt(ss * (1.0 / B_HD) + EPS) * gain_tiled


def _shared_kv_kernel(x_ref, norm_ref, wkv_ref, kgain_ref, k_ref, v_ref):
    hb = _rms_rows(x_ref[0], norm_ref[...]).astype(BF16)
    kv = jnp.dot(hb, wkv_ref[...], preferred_element_type=F32)
    k_ref[0] = _head_rms(kv[:, :B_KV_W], kgain_ref[...])
    v_ref[0] = kv[:, B_KV_W:]


def _shared_kv(x, kv_norm, w_kv, k_gain):
    b, t, d = x.shape
    tk = min(KV_TILE, t)
    out = jax.ShapeDtypeStruct((b, t, B_KV_W), F32)
    spec_out = pl.BlockSpec((1, tk, B_KV_W), lambda bi, ti: (bi, ti, 0))
    return pl.pallas_call(
        _shared_kv_kernel,
        out_shape=(out, out),
        grid=(b, t // tk),
        in_specs=[pl.BlockSpec((1, tk, d), lambda bi, ti: (bi, ti, 0)),
                  _const_spec((1, d)), _const_spec((d, 2 * B_KV_W)), _const_spec((1, B_KV_W))],
        out_specs=(spec_out, spec_out),
        compiler_params=pltpu.CompilerParams(
            dimension_semantics=("arbitrary", "arbitrary"), vmem_limit_bytes=VMEM_LIMIT),
        name="shared_kv",
    )(x, kv_norm.reshape(1, d), w_kv.astype(BF16), jnp.tile(k_gain, B_KV_HEADS).reshape(1, B_KV_W))


def _bucket_ranges():
    dist = np.arange(WINDOW)
    max_exact = N_BUCKETS // 2
    large = max_exact + (np.log(np.maximum(dist, 1) / max_exact) / np.log(MAX_DIST / max_exact)
                         * (N_BUCKETS - max_exact)).astype(np.int64)
    large = np.minimum(large, N_BUCKETS - 1)
    bucket = np.where(dist < max_exact, dist, large)
    ranges = []
    for bkt in range(N_BUCKETS):
        idx = np.nonzero(bucket == bkt)[0]
        if idx.size:
            assert np.all(np.diff(idx) == 1)
            ranges.append((bkt, int(idx[0]), int(idx[-1])))
    return ranges


def _band_bias_kernel(rel_ref, out_ref):
    h = pl.program_id(0)
    qi = lax.broadcasted_iota(jnp.int32, (BLOCK, BLOCK), 0)
    si = lax.broadcasted_iota(jnp.int32, (BLOCK, BLOCK), 1)
    dist = jnp.where(si > qi, qi - si + BLOCK, qi - si)
    ranges = _bucket_ranges()
    assert ranges[0][1] == 0 and ranges[-1][2] == WINDOW - 1
    acc = jnp.zeros((BLOCK, BLOCK), F32)
    for bkt, lo, hi in ranges:
        acc = jnp.where((dist >= lo) & (dist <= hi), rel_ref[bkt, h] * LOG2E, acc)
    out_ref[0] = acc


def _band_bias(rel_bias):
    return pl.pallas_call(
        _band_bias_kernel,
        out_shape=jax.ShapeDtypeStruct((B_Q_HEADS, BLOCK, BLOCK), F32),
        grid=(B_Q_HEADS,),
        in_specs=[pl.BlockSpec(memory_space=pltpu.SMEM)],
        out_specs=pl.BlockSpec((1, BLOCK, BLOCK), lambda h: (h, 0, 0)),
        compiler_params=pltpu.CompilerParams(dimension_semantics=("arbitrary",)),
        name="band_bias",
    )(rel_bias)


def _swa_kernel(layer, sinks_ref, x_ref, norm_ref, win_ref, qgain_ref,
                kprev_ref, kcur_ref, vprev_ref, vcur_ref, bias_ref, wout_ref,
                out_ref,
                q_ref, ka_ref, kb_ref, va_ref, vb_ref, o_ref, gate_ref):
    tb = B_TILE
    first = pl.program_id(1) == 0
    x = x_ref[0]
    hb = _rms_rows(x, norm_ref[0]).astype(BF16)
    q = jnp.dot(hb, win_ref[0, :, 0:B_W], preferred_element_type=F32)
    q_ref[...] = _head_rms(q, qgain_ref[0] * (B_HD ** -0.5 * LOG2E)).astype(BF16)
    sinks = [sinks_ref[layer, hq] * LOG2E for hq in range(B_Q_HEADS)]

    def gate_proj(p):
        wz = win_ref[0, :, B_W + p * GATE_SLAB:B_W + (p + 1) * GATE_SLAB]
        gate_ref[:, p * GATE_SLAB:(p + 1) * GATE_SLAB] = _silu(jnp.dot(hb, wz, preferred_element_type=F32))

    lane = lax.broadcasted_iota(jnp.int32, (BLOCK + tb, LANES), 1)
    low_half = lane < B_HD
    for t2 in range(B_KV_W // LANES):
        cols = slice(t2 * LANES, (t2 + 1) * LANES)
        for src_prev, src_cur, dst_a, dst_b in ((kprev_ref, kcur_ref, ka_ref, kb_ref),
                                                (vprev_ref, vcur_ref, va_ref, vb_ref)):
            full = jnp.concatenate([src_prev[0, :, cols], src_cur[0, :, cols]], axis=0)
            swapped = pltpu.roll(full, B_HD, 1)
            zero = jnp.zeros_like(full)
            dst_a[2 * t2] = jnp.where(low_half, full, zero).astype(BF16)
            dst_b[2 * t2] = jnp.where(low_half, zero, swapped).astype(BF16)
            dst_a[2 * t2 + 1] = jnp.where(low_half, swapped, zero).astype(BF16)
            dst_b[2 * t2 + 1] = jnp.where(low_half, zero, full).astype(BF16)

    lane_q = lax.broadcasted_iota(jnp.int32, (BLOCK, LANES), 1)
    from_prev = (lax.broadcasted_iota(jnp.int32, (BLOCK, BLOCK), 1)
                 > lax.broadcasted_iota(jnp.int32, (BLOCK, BLOCK), 0))
    pen = jnp.where(first & from_prev, NEG, 0.0)

    groups = [(qb, h0) for qb in range(tb // BLOCK) for h0 in range(0, B_Q_HEADS, SWA_HEAD_GROUP)]

    def qk_logits(qb, h0):
        rows = slice(qb * BLOCK, (qb + 1) * BLOCK)
        band = slice(qb * BLOCK, qb * BLOCK + 2 * BLOCK)
        j = h0 // B_GROUP
        kz = (ka_ref[j, band, :], kb_ref[j, band, :])
        out = {}
        for hq in range(h0, h0 + SWA_HEAD_GROUP):
            qpair = q_ref[rows, (hq // 2) * LANES:(hq // 2 + 1) * LANES]
            lg2 = lax.dot_general(qpair, kz[hq % 2], _NT, preferred_element_type=F32)
            lg = jnp.where(from_prev, lg2[:, 0:BLOCK], lg2[:, BLOCK:2 * BLOCK]) + bias_ref[hq]
            out[hq] = lg + pen if qb == 0 else lg
        return out

    def softmax_pv(qb, h0, logits):
        rows = slice(qb * BLOCK, (qb + 1) * BLOCK)
        band = slice(qb * BLOCK, qb * BLOCK + 2 * BLOCK)
        j = h0 // B_GROUP
        vz = (va_ref[j, band, :], vb_ref[j, band, :])
        heads = range(h0, h0 + SWA_HEAD_GROUP)
        m, pexp, inv = {}, {}, {}
        for hq in heads:
            m[hq] = jnp.maximum(jnp.max(logits[hq], axis=-1, keepdims=True), sinks[hq])
        for hq in heads:
            pe = jnp.exp2(logits[hq] - m[hq])
            denom = jnp.sum(pe, axis=-1, keepdims=True) + jnp.exp2(sinks[hq] - m[hq])
            inv[hq] = 1.0 / denom
            pexp[hq] = jnp.concatenate([jnp.where(from_prev, pe, 0.0).astype(BF16),
                                        jnp.where(from_prev, 0.0, pe).astype(BF16)], axis=1)
        for hq in heads[::2]:
            pv = (jnp.dot(pexp[hq], vz[0], preferred_element_type=F32)
                  + jnp.dot(pexp[hq + 1], vz[1], preferred_element_type=F32))
            p = hq // 2
            o_ref[rows, p * LANES:(p + 1) * LANES] = pv * jnp.where(lane_q < B_HD, inv[hq], inv[hq + 1])

    n_gate = B_W // GATE_SLAB
    logits = qk_logits(*groups[0])
    for gi, grp in enumerate(groups):
        nxt = qk_logits(*groups[gi + 1]) if gi + 1 < len(groups) else None
        softmax_pv(*grp, logits)
        for p in range(gi * n_gate // len(groups), (gi + 1) * n_gate // len(groups)):
            gate_proj(p)
        logits = nxt

    og = (o_ref[...] * gate_ref[...]).astype(BF16)
    out_ref[0] = x + jnp.dot(og, wout_ref[0], preferred_element_type=F32)


def _swa_layer(x, layer, norm, w_in, q_gain_tiled, sinks, w_out, k, v, bias):
    b, t, d = x.shape
    tb = B_TILE
    nblk = tb // BLOCK
    n = norm.shape[0]
    tile = lambda bi, ti: (bi, ti, 0)
    prev = lambda bi, ti: (bi, jnp.maximum(ti * nblk - 1, 0), 0)
    in_specs = [
        pl.BlockSpec(memory_space=pltpu.SMEM),
        pl.BlockSpec((1, tb, d), tile),
        _layer_spec((1, d), layer),
        _layer_spec((d, 2 * B_W), layer),
        _layer_spec((1, B_W), layer),
        pl.BlockSpec((1, BLOCK, B_KV_W), prev),
        pl.BlockSpec((1, tb, B_KV_W), tile),
        pl.BlockSpec((1, BLOCK, B_KV_W), prev),
        pl.BlockSpec((1, tb, B_KV_W), tile),
        _const_spec((B_Q_HEADS, BLOCK, BLOCK)),
        _layer_spec((B_W, d), layer),
    ]
    kv_scratch = pltpu.VMEM((B_KV_HEADS, BLOCK + tb, LANES), BF16)
    scratch = [pltpu.VMEM((tb, B_W), BF16), kv_scratch, kv_scratch, kv_scratch, kv_scratch,
               pltpu.VMEM((tb, B_W), F32), pltpu.VMEM((tb, B_W), F32)]
    return pl.pallas_call(
        functools.partial(_swa_kernel, layer),
        out_shape=jax.ShapeDtypeStruct((b, t, d), F32),
        grid=(b, t // tb),
        in_specs=in_specs,
        out_specs=pl.BlockSpec((1, tb, d), tile),
        scratch_shapes=scratch,
        compiler_params=pltpu.CompilerParams(
            dimension_semantics=("arbitrary", "arbitrary"), vmem_limit_bytes=VMEM_LIMIT),
        name="swa_layer",
    )(sinks, x, norm.reshape(n, 1, d), w_in, q_gain_tiled, k, k, v, v, bias, w_out)


def kernel(x, a_norm, a_w_in, a_conv, a_A_log, a_dt_bias, a_o_gain, a_w_out, kv_norm, w_kv, k_gain, rel_bias, b_norm, b_w_in, b_q_gain, b_sinks, b_w_out):
    n_a = a_w_in.shape[0]
    n_b = b_w_in.shape[0]
    a_w_in_b = a_w_in.astype(BF16)
    a_w_ab_t = jnp.swapaxes(a_w_in_b[:, :, A_CONV_CH + A_V_W:], 1, 2)
    a_w_out_b = a_w_out.astype(BF16)
    b_w_in_b = b_w_in.astype(BF16)
    b_w_out_b = b_w_out.astype(BF16)
    q_gain_tiled = jnp.tile(b_q_gain, (1, B_Q_HEADS)).reshape(n_b, 1, B_W)
    for i in range(n_a):
        x = _deltanet_layer(x, i, a_norm, a_w_in_b, a_w_ab_t, a_conv, a_A_log, a_dt_bias,
                            a_o_gain, a_w_out_b)
    k, v = _shared_kv(x, kv_norm, w_kv, k_gain)
    bias = _band_bias(rel_bias)
    for j in range(n_b):
        x = _swa_layer(x, j, b_norm, b_w_in_b, q_gain_tiled, b_sinks, b_w_out_b, k, v, bias)
    return x
```

```python
import functools
import math

import numpy as np
import jax
import jax.numpy as jnp
from jax import lax
from jax.experimental import pallas as pl
from jax.experimental.pallas import tpu as pltpu

F32 = jnp.float32
BF16 = jnp.bfloat16

D_MODEL = 1024
EPS = 1e-6

A_HEADS = 8
A_DK = 128
A_DV = 256
A_QK_W = A_HEADS * A_DK
A_V_W = A_HEADS * A_DV
A_CONV_CH = 2 * A_QK_W + A_V_W
A_IN_W = A_CONV_CH + A_V_W + 2 * A_HEADS
CONV_W = 4
CHUNK = 64
A_TILE = 256
CONV_PAD = 8
PROJ_SLAB = 512
SBS_HEADS = 4

B_Q_HEADS = 32
B_KV_HEADS = 4
B_GROUP = B_Q_HEADS // B_KV_HEADS
B_HD = 64
B_W = B_Q_HEADS * B_HD
B_KV_W = B_KV_HEADS * B_HD
WINDOW = 128
BLOCK = 128
B_TILE = 512
SWA_HEAD_GROUP = 8
KV_TILE = 512
N_BUCKETS = 32
MAX_DIST = 128
NEG = -1e30
LOG2E = math.log2(math.e)
GATE_SLAB = 256

LANES = 128
VMEM_LIMIT = 56 * 1024 * 1024

_NT = (((1,), (1,)), ((), ()))
_NN = (((1,), (0,)), ((), ()))
_TN = (((0,), (0,)), ((), ()))


def _mm(a, b, dims=_NN):
    return lax.dot_general(a.astype(BF16), b.astype(BF16), dims, preferred_element_type=F32)


def _split3(a):
    hi = a.astype(BF16)
    rest = a - hi.astype(F32)
    mid = rest.astype(BF16)
    lo = (rest - mid.astype(F32)).astype(BF16)
    return hi, mid, lo


def _silu(x):
    return x * jax.nn.sigmoid(x)


def _rms_rows(x, gain):
    return x * lax.rsqrt(jnp.mean(x * x, axis=-1, keepdims=True) + EPS) * gain


def _const_spec(shape):
    nd = len(shape)
    return pl.BlockSpec(shape, lambda *_: (0,) * nd, pipeline_mode=pl.Buffered(1))


def _layer_spec(shape, layer):
    nd = len(shape)
    return pl.BlockSpec((1,) + tuple(shape), lambda *_: (layer,) + (0,) * nd,
                        pipeline_mode=pl.Buffered(1))


def _deltanet_kernel(x_ref, norm_ref, win_ref, wab_ref, wabt_ref, conv_ref,
                     alog_ref, dtb_ref, alog_c_ref, dtb_c_ref, ogain_ref, wout_ref,
                     out_ref, *scratch):
    n_slabs = A_CONV_CH // PROJ_SLAB
    ext_refs = scratch[:n_slabs]
    state_ref, q_ref, k_ref, v_ref, gate_ref, og_ref = scratch[n_slabs:]
    tm = A_TILE
    c = CHUNK
    sbs = SBS_HEADS
    n_chunks = tm // c
    log_c = int(math.log2(c))
    assert n_chunks == sbs
    first = pl.program_id(1) == 0

    @pl.when(first)
    def _():
        for ext_ref in ext_refs:
            ext_ref[0:CONV_PAD, :] = jnp.zeros((CONV_PAD, PROJ_SLAB), F32)
        state_ref[...] = jnp.zeros_like(state_ref)

    x = x_ref[0]
    hb = _rms_rows(x, norm_ref[0]).astype(BF16)
    z_col = A_CONV_CH


    def project(p):
        ext_refs[p][CONV_PAD:CONV_PAD + tm, :] = _mm(hb, win_ref[0, :, p * PROJ_SLAB:(p + 1) * PROJ_SLAB])

    def conv_slab(s):
        cols = slice(s * LANES, (s + 1) * LANES)
        p, off = divmod(s * LANES, PROJ_SLAB)
        ext = ext_refs[p][:, off:off + LANES]
        acc = ext * conv_ref[0, 0:1, cols]
        for j in range(1, CONV_W):
            acc = ext * conv_ref[0, j:j + 1, cols] + pltpu.roll(acc, 1, 0)
        y = _silu(acc[CONV_PAD:CONV_PAD + tm])
        if s < A_HEADS:
            y = y * (lax.rsqrt(jnp.sum(y * y, axis=-1, keepdims=True) + EPS) * (A_DK ** -0.5))
            q_ref[:, cols] = y
        elif s < 2 * A_HEADS:
            y = y * lax.rsqrt(jnp.sum(y * y, axis=-1, keepdims=True) + EPS)
            k_ref[:, s * LANES - A_QK_W:(s + 1) * LANES - A_QK_W] = y
        else:
            v_ref[:, s * LANES - 2 * A_QK_W:(s + 1) * LANES - 2 * A_QK_W] = y

    def gate_proj(h):
        wz = win_ref[0, :, z_col + h * A_DV:z_col + (h + 1) * A_DV]
        gate_ref[:, h * A_DV:(h + 1) * A_DV] = _silu(_mm(hb, wz))

    ab = _mm(hb, wab_ref[0])
    ab_t = lax.dot_general(wabt_ref[0], hb, _NT, preferred_element_type=F32)

    qk_slabs = 2 * A_QK_W // PROJ_SLAB
    for p in range(qk_slabs // 2):
        project(p)

    beta_c = jax.nn.sigmoid(ab[:, 0:A_HEADS])
    g_c = -jnp.exp(alog_ref[0]) * jax.nn.softplus(ab[:, A_HEADS:2 * A_HEADS] + dtb_ref[0])
    g_r = -jnp.exp(alog_c_ref[0]) * jax.nn.softplus(ab_t[A_HEADS:2 * A_HEADS, :] + dtb_c_ref[0])

    ri = lax.broadcasted_iota(jnp.int32, (tm, tm), 0)
    ci = lax.broadcasted_iota(jnp.int32, (tm, tm), 1)
    same = (ri >> log_c) == (ci >> log_c)
    sums_c = jnp.concatenate([(same & (ri >= ci)).astype(BF16), same.astype(BF16)], axis=0)
    sums_c = sum(jnp.dot(sums_c, part, preferred_element_type=F32) for part in _split3(g_c))
    gcum_c = sums_c[0:tm]
    gtot_c = sums_c[tm:2 * tm]
    gcum_r = jnp.dot(jnp.concatenate(_split3(g_r), axis=0), (same & (ri <= ci)).astype(BF16),
                     preferred_element_type=F32)
    gcum_r = gcum_r[0:A_HEADS] + gcum_r[A_HEADS:2 * A_HEADS] + gcum_r[2 * A_HEADS:3 * A_HEADS]
    gcum_rot = [gcum_r] + [pltpu.roll(gcum_r, k * c, 1) for k in range(1, n_chunks)]

    for p in range(qk_slabs // 2, qk_slabs):
        project(p)
    for s in range(2 * A_HEADS):
        conv_slab(s)
    for p in range(qk_slabs, n_slabs):
        project(p)
    for h in range(A_HEADS // 2):
        gate_proj(h)
    for s in range(2 * A_HEADS, A_CONV_CH // LANES):
        conv_slab(s)
    for ext_ref in ext_refs:
        ext_ref[0:CONV_PAD, :] = ext_ref[tm:tm + CONV_PAD, :]

    sw = sbs * c
    row_i = lax.broadcasted_iota(jnp.int32, (c, sw), 0)
    lane_i = lax.broadcasted_iota(jnp.int32, (c, sw), 1)
    pos_i = lane_i & (c - 1)
    blk_i = lane_i >> log_c
    incl = row_i >= pos_i
    strict = row_i > pos_i
    eye = (row_i == pos_i).astype(F32)
    bd_p = ((lax.broadcasted_iota(jnp.int32, (sw, sw), 0) >> log_c)
            == (lax.broadcasted_iota(jnp.int32, (sw, sw), 1) >> log_c))
    bd_k = ((lax.broadcasted_iota(jnp.int32, (sw, sbs * A_DK), 0) >> log_c)
            == (lax.broadcasted_iota(jnp.int32, (sw, sbs * A_DK), 1) >> int(math.log2(A_DK))))

    def spread(cols):
        out = cols[:, sbs - 1:sbs]
        for hp in range(sbs - 2, -1, -1):
            out = jnp.where(blk_i == hp, cols[:, hp:hp + 1], out)
        return out

    def block_diag(m, mask):
        return jnp.where(mask, jnp.concatenate([m] * sbs, axis=0), jnp.zeros((), m.dtype))

    units =[(g, j) for g in range(A_HEADS // sbs) for j in range(n_chunks)]
    a_qk, t_inv, pw = {}, {}, {}
    for g, j in units:
        rows = slice(j * c, (j + 1) * c)
        hcols = slice(g * sbs * A_DK, (g + 1) * sbs * A_DK)
        qg = q_ref[rows, hcols].astype(BF16)
        kg = k_ref[rows, hcols].astype(BF16)
        qk_kk = lax.dot_general(jnp.concatenate([qg, kg], axis=0), block_diag(kg, bd_k), _NT,
                                preferred_element_type=F32)
        gcol = spread(gcum_c[rows, g * sbs:(g + 1) * sbs])
        grow = gcum_rot[(sbs - 1 - j) % n_chunks][g * sbs + sbs - 1:g * sbs + sbs, :]
        for hp in range(sbs - 2, -1, -1):
            grow = jnp.where(blk_i[0:1] == hp,
                             gcum_rot[(hp - j) % n_chunks][g * sbs + hp:g * sbs + hp + 1, :], grow)
        decay = jnp.where(incl, jnp.exp(jnp.minimum(gcol - grow, 0.0)), 0.0)
        a_qk[g, j] = qk_kk[0:c] * decay
        low = jnp.where(strict, qk_kk[c:2 * c] * decay * spread(beta_c[rows, g * sbs:(g + 1) * sbs]), 0.0)
        t_inv[g, j] = eye - low
        pw[g, j] = (-low).astype(BF16)

    for rnd in range(log_c):
        if 1 <= rnd <= A_HEADS // 2:
            gate_proj(A_HEADS // 2 + rnd - 1)
        for u in units:
            bd = block_diag(pw[u], bd_p)
            if rnd == 0:
                pw[u] = jnp.dot(pw[u], bd, preferred_element_type=F32).astype(BF16)
            elif rnd == log_c - 1:
                t_inv[u] = t_inv[u] + jnp.dot(t_inv[u].astype(BF16), bd, preferred_element_type=F32)
            else:
                res = jnp.dot(jnp.concatenate([pw[u], t_inv[u].astype(BF16)], axis=0), bd,
                              preferred_element_type=F32)
                pw[u] = res[0:c].astype(BF16)
                t_inv[u] = t_inv[u] + res[c:2 * c]

    uw, a_h, q_dec, k_dec = {}, {}, {}, {}
    for h in range(A_HEADS):
        g, hp = divmod(h, sbs)
        qh = q_ref[:, h * A_DK:(h + 1) * A_DK]
        kh = k_ref[:, h * A_DK:(h + 1) * A_DK]
        vh = v_ref[:, h * A_DV:(h + 1) * A_DV]
        gc = gcum_c[:, h:h + 1]
        bc = beta_c[:, h:h + 1]
        eg = jnp.exp(gc)
        rhs = jnp.concatenate([(vh * bc).astype(BF16), (kh * (bc * eg)).astype(BF16)], axis=1)
        q_dec[h] = (qh * eg).astype(BF16)
        k_dec[h] = (kh * jnp.exp(gtot_c[:, h:h + 1] - gc)).astype(BF16)
        for j in range(n_chunks):
            rows = slice(j * c, (j + 1) * c)
            t_hc = t_inv[g, j][:, hp * c:(hp + 1) * c].astype(BF16)
            uw[h, j] = jnp.dot(t_hc, rhs[rows], preferred_element_type=F32)
            a_h[h, j] = a_qk[g, j][:, hp * c:(hp + 1) * c].astype(BF16)

    st = {h: state_ref[h] for h in range(A_HEADS)}
    o_parts = {h: [] for h in range(A_HEADS)}
    for j in range(n_chunks):
        rows = slice(j * c, (j + 1) * c)
        wq = {}
        for h in range(A_HEADS):
            w = uw[h, j][:, A_DV:].astype(BF16)
            wq[h] = jnp.dot(jnp.concatenate([w, q_dec[h][rows]], axis=0), st[h].astype(BF16),
                            preferred_element_type=F32)
        for h in range(A_HEADS):
            v_new = (uw[h, j][:, 0:A_DV] - wq[h][0:c]).astype(BF16)
            o_parts[h].append(wq[h][c:2 * c] + jnp.dot(a_h[h, j], v_new, preferred_element_type=F32))
            g_last = jnp.exp(gtot_c[j * c:j * c + 1, h:h + 1])
            st[h] = st[h] * g_last + lax.dot_general(k_dec[h][rows], v_new, _TN,
                                                     preferred_element_type=F32)

    for h in range(A_HEADS):
        state_ref[h] = st[h]
        o = _rms_rows(jnp.concatenate(o_parts[h], axis=0), ogain_ref[0])
        og_ref[:, h * A_DV:(h + 1) * A_DV] = (o * gate_ref[:, h * A_DV:(h + 1) * A_DV]).astype(BF16)

    out_ref[0] = x + jnp.dot(og_ref[...], wout_ref[0], preferred_element_type=F32)


def _deltanet_layer(x, layer, norm, w_in, w_ab, w_ab_t, conv_w, a_log, dt_bias, o_gain, w_out):
    b, t, d = x.shape
    tm = A_TILE
    n = norm.shape[0]
    in_specs = [
        pl.BlockSpec((1, tm, d), lambda bi, ti: (bi, ti, 0)),
        _layer_spec((1, d), layer),
        _layer_spec((d, A_CONV_CH + A_V_W), layer),
        _layer_spec((d, 2 * A_HEADS), layer),
        _layer_spec((2 * A_HEADS, d), layer),
        _layer_spec((CONV_W, A_CONV_CH), layer),
        _layer_spec((1, A_HEADS), layer),
        _layer_spec((1, A_HEADS), layer),
        _layer_spec((A_HEADS, 1), layer),
        _layer_spec((A_HEADS, 1), layer),
        _layer_spec((1, A_DV), layer),
        _layer_spec((A_V_W, d), layer),
    ]
    scratch = [pltpu.VMEM((tm + CONV_PAD, PROJ_SLAB), F32)
               for _ in range(A_CONV_CH // PROJ_SLAB)] + [
        pltpu.VMEM((A_HEADS, A_DK, A_DV), F32),
        pltpu.VMEM((tm, A_QK_W), F32),
        pltpu.VMEM((tm, A_QK_W), F32),
        pltpu.VMEM((tm, A_V_W), F32),
        pltpu.VMEM((tm, A_V_W), F32),
        pltpu.VMEM((tm, A_V_W), BF16),
    ]
    return pl.pallas_call(
        _deltanet_kernel,
        out_shape=jax.ShapeDtypeStruct((b, t, d), F32),
        grid=(b, t // tm),
        in_specs=in_specs,
        out_specs=pl.BlockSpec((1, tm, d), lambda bi, ti: (bi, ti, 0)),
        scratch_shapes=scratch,
        compiler_params=pltpu.CompilerParams(
            dimension_semantics=("arbitrary", "arbitrary"), vmem_limit_bytes=VMEM_LIMIT),
        name="deltanet_layer",
    )(x, norm.reshape(n, 1, d), w_in, w_ab, w_ab_t, conv_w,
      a_log.reshape(n, 1, A_HEADS), dt_bias.reshape(n, 1, A_HEADS),
      a_log.reshape(n, A_HEADS, 1), dt_bias.reshape(n, A_HEADS, 1),
      o_gain.reshape(n, 1, A_DV), w_out)


def _head_sum_matrix(width):
    r = lax.broadcasted_iota(jnp.int32, (width, width), 0) // B_HD
    c = lax.broadcasted_iota(jnp.int32, (width, width), 1) // B_HD
    return (r == c).astype(BF16)


def _head_rms(x, gain_tiled):
    width = x.shape[-1]
    summer = _head_sum_matrix(LANES)
    sq = x * x
    hi = sq.astype(BF16)
    lo = (sq - hi.astype(F32)).astype(BF16)
    parts = []
    for s in range(width // LANES):
        cols = slice(s * LANES, (s + 1) * LANES)
        parts.append(jnp.dot(hi[:, cols], summer, preferred_element_type=F32)
                     + jnp.dot(lo[:, cols], summer, preferred_element_type=F32))
    ss = jnp.concatenate(parts, axis=-1) if len(parts) > 1 else parts[0]
    return x * lax.rsqrt(ss * (1.0 / B_HD) + EPS) * gain_tiled


def _shared_kv_kernel(x_ref, norm_ref, wkv_ref, kgain_ref, k_ref, v_ref):
    hb = _rms_rows(x_ref[0], norm_ref[...]).astype(BF16)
    kv = jnp.dot(hb, wkv_ref[...], preferred_element_type=F32)
    k_ref[0] = _head_rms(kv[:, :B_KV_W], kgain_ref[...])
    v_ref[0] = kv[:, B_KV_W:]


def _shared_kv(x, kv_norm, w_kv, k_gain):
    b, t, d = x.shape
    tk = min(KV_TILE, t)
    out = jax.ShapeDtypeStruct((b, t, B_KV_W), F32)
    spec_out = pl.BlockSpec((1, tk, B_KV_W), lambda bi, ti: (bi, ti, 0))
    return pl.pallas_call(
        _shared_kv_kernel,
        out_shape=(out, out),
        grid=(b, t // tk),
        in_specs=[pl.BlockSpec((1, tk, d), lambda bi, ti: (bi, ti, 0)),
                  _const_spec((1, d)), _const_spec((d, 2 * B_KV_W)), _const_spec((1, B_KV_W))],
        out_specs=(spec_out, spec_out),
        compiler_params=pltpu.CompilerParams(
            dimension_semantics=("arbitrary", "arbitrary"), vmem_limit_bytes=VMEM_LIMIT),
        name="shared_kv",
    )(x, kv_norm.reshape(1, d), w_kv.astype(BF16), jnp.tile(k_gain, B_KV_HEADS).reshape(1, B_KV_W))


def _bucket_ranges():
    dist = np.arange(WINDOW)
    max_exact = N_BUCKETS // 2
    large = max_exact + (np.log(np.maximum(dist, 1) / max_exact) / np.log(MAX_DIST / max_exact)
                         * (N_BUCKETS - max_exact)).astype(np.int64)
    large = np.minimum(large, N_BUCKETS - 1)
    bucket = np.where(dist < max_exact, dist, large)
    ranges = []
    for bkt in range(N_BUCKETS):
        idx = np.nonzero(bucket == bkt)[0]
        if idx.size:
            assert np.all(np.diff(idx) == 1)
            ranges.append((bkt, int(idx[0]), int(idx[-1])))
    return ranges


def _band_bias_kernel(rel_ref, out_ref):
    h = pl.program_id(0)
    qi = lax.broadcasted_iota(jnp.int32, (BLOCK, BLOCK), 0)
    si = lax.broadcasted_iota(jnp.int32, (BLOCK, BLOCK), 1)
    dist = jnp.where(si > qi, qi - si + BLOCK, qi - si)
    ranges = _bucket_ranges()
    assert ranges[0][1] == 0 and ranges[-1][2] == WINDOW - 1
    acc = jnp.zeros((BLOCK, BLOCK), F32)
    for bkt, lo, hi in ranges:
        acc = jnp.where((dist >= lo) & (dist <= hi), rel_ref[bkt, h] * LOG2E, acc)
    out_ref[0] = acc


def _band_bias(rel_bias):
    return pl.pallas_call(
        _band_bias_kernel,
        out_shape=jax.ShapeDtypeStruct((B_Q_HEADS, BLOCK, BLOCK), F32),
        grid=(B_Q_HEADS,),
        in_specs=[pl.BlockSpec(memory_space=pltpu.SMEM)],
        out_specs=pl.BlockSpec((1, BLOCK, BLOCK), lambda h: (h, 0, 0)),
        compiler_params=pltpu.CompilerParams(dimension_semantics=("arbitrary",)),
        name="band_bias",
    )(rel_bias)


def _swa_kernel(layer, sinks_ref, x_ref, norm_ref, win_ref, qgain_ref,
                kprev_ref, kcur_ref, vprev_ref, vcur_ref, bias_ref, wout_ref,
                out_ref,
                q_ref, ka_ref, kb_ref, va_ref, vb_ref, o_ref, gate_ref):
    tb = B_TILE
    first = pl.program_id(1) == 0
    x = x_ref[0]
    hb = _rms_rows(x, norm_ref[0]).astype(BF16)
    sinks = [sinks_ref[layer, hq] * LOG2E for hq in range(B_Q_HEADS)]
    group_w = SWA_HEAD_GROUP * B_HD

    def q_proj(s):
        cols = slice(s * group_w, (s + 1) * group_w)
        q = jnp.dot(hb, win_ref[0, :, cols], preferred_element_type=F32)
        q_ref[:, cols] = _head_rms(q, qgain_ref[0, :, cols] * (B_HD ** -0.5 * LOG2E)).astype(BF16)

    def gate_proj(p):
        wz = win_ref[0, :, B_W + p * GATE_SLAB:B_W + (p + 1) * GATE_SLAB]
        gate_ref[:, p * GATE_SLAB:(p + 1) * GATE_SLAB] = _silu(jnp.dot(hb, wz, preferred_element_type=F32))

    lane = lax.broadcasted_iota(jnp.int32, (BLOCK + tb, LANES), 1)
    low_half = lane < B_HD
    for t2 in range(B_KV_W // LANES):
        cols = slice(t2 * LANES, (t2 + 1) * LANES)
        for src_prev, src_cur, dst_a, dst_b in ((kprev_ref, kcur_ref, ka_ref, kb_ref),
                                                (vprev_ref, vcur_ref, va_ref, vb_ref)):
            full = jnp.concatenate([src_prev[0, :, cols], src_cur[0, :, cols]], axis=0)
            swapped = pltpu.roll(full, B_HD, 1)
            zero = jnp.zeros_like(full)
            dst_a[2 * t2] = jnp.where(low_half, full, zero).astype(BF16)
            dst_b[2 * t2] = jnp.where(low_half, zero, swapped).astype(BF16)
            dst_a[2 * t2 + 1] = jnp.where(low_half, swapped, zero).astype(BF16)
            dst_b[2 * t2 + 1] = jnp.where(low_half, zero, full).astype(BF16)

    lane_q = lax.broadcasted_iota(jnp.int32, (BLOCK, LANES), 1)
    from_prev = (lax.broadcasted_iota(jnp.int32, (BLOCK, BLOCK), 1)
                 > lax.broadcasted_iota(jnp.int32, (BLOCK, BLOCK), 0))
    pen = jnp.where(first & from_prev, NEG, 0.0)

    groups = [(qb, h0) for qb in range(tb // BLOCK) for h0 in range(0, B_Q_HEADS, SWA_HEAD_GROUP)]

    def qk_logits(qb, h0):
        rows = slice(qb * BLOCK, (qb + 1) * BLOCK)
        band = slice(qb * BLOCK, qb * BLOCK + 2 * BLOCK)
        j = h0 // B_GROUP
        kz = (ka_ref[j, band, :], kb_ref[j, band, :])
        out = {}
        for hq in range(h0, h0 + SWA_HEAD_GROUP):
            qpair = q_ref[rows, (hq // 2) * LANES:(hq // 2 + 1) * LANES]
            lg2 = lax.dot_general(qpair, kz[hq % 2], _NT, preferred_element_type=F32)
            lg = jnp.where(from_prev, lg2[:, 0:BLOCK], lg2[:, BLOCK:2 * BLOCK]) + bias_ref[hq]
            out[hq] = lg + pen if qb == 0 else lg
        return out

    def softmax_pv(qb, h0, logits):
        rows = slice(qb * BLOCK, (qb + 1) * BLOCK)
        band = slice(qb * BLOCK, qb * BLOCK + 2 * BLOCK)
        j = h0 // B_GROUP
        vz = (va_ref[j, band, :], vb_ref[j, band, :])
        heads = range(h0, h0 + SWA_HEAD_GROUP)
        m, pexp, inv = {}, {}, {}
        for hq in heads:
            m[hq] = jnp.maximum(jnp.max(logits[hq], axis=-1, keepdims=True), sinks[hq])
        for hq in heads:
            pe = jnp.exp2(logits[hq] - m[hq])
            denom = jnp.sum(pe, axis=-1, keepdims=True) + jnp.exp2(sinks[hq] - m[hq])
            inv[hq] = 1.0 / denom
            pexp[hq] = jnp.concatenate([jnp.where(from_prev, pe, 0.0).astype(BF16),
                                        jnp.where(from_prev, 0.0, pe).astype(BF16)], axis=1)
        for hq in heads[::2]:
            pv = (jnp.dot(pexp[hq], vz[0], preferred_element_type=F32)
                  + jnp.dot(pexp[hq + 1], vz[1], preferred_element_type=F32))
            p = hq // 2
            o_ref[rows, p * LANES:(p + 1) * LANES] = pv * jnp.where(lane_q < B_HD, inv[hq], inv[hq + 1])

    n_gate = B_W // GATE_SLAB
    groups_per_block = B_Q_HEADS // SWA_HEAD_GROUP
    q_proj(0)
    logits = qk_logits(*groups[0])
    for gi, grp in enumerate(groups):
        if gi + 1 < groups_per_block:
            q_proj(gi + 1)
        nxt = qk_logits(*groups[gi + 1]) if gi + 1 < len(groups) else None
        softmax_pv(*grp, logits)
        for p in range(gi * n_gate // groups_per_block, min(n_gate, (gi + 1) * n_gate // groups_per_block)):
            gate_proj(p)
        if (gi + 1) % groups_per_block == 0:
            rows = slice(grp[0] * BLOCK, (grp[0] + 1) * BLOCK)
            og = (o_ref[rows, :] * gate_ref[rows, :]).astype(BF16)
            out_ref[0, rows, :] = x_ref[0, rows, :] + jnp.dot(og, wout_ref[0], preferred_element_type=F32)
        logits = nxt


def _swa_layer(x, layer, norm, w_in, q_gain_tiled, sinks, w_out, k, v, bias):
    b, t, d = x.shape
    tb = B_TILE
    nblk = tb // BLOCK
    n = norm.shape[0]
    tile = lambda bi, ti: (bi, ti, 0)
    prev = lambda bi, ti: (bi, jnp.maximum(ti * nblk - 1, 0), 0)
    in_specs = [
        pl.BlockSpec(memory_space=pltpu.SMEM),
        pl.BlockSpec((1, tb, d), tile),
        _layer_spec((1, d), layer),
        _layer_spec((d, 2 * B_W), layer),
        _layer_spec((1, B_W), layer),
        pl.BlockSpec((1, BLOCK, B_KV_W), prev),
        pl.BlockSpec((1, tb, B_KV_W), tile),
        pl.BlockSpec((1, BLOCK, B_KV_W), prev),
        pl.BlockSpec((1, tb, B_KV_W), tile),
        _const_spec((B_Q_HEADS, BLOCK, BLOCK)),
        _layer_spec((B_W, d), layer),
    ]
    kv_scratch = pltpu.VMEM((B_KV_HEADS, BLOCK + tb, LANES), BF16)
    scratch = [pltpu.VMEM((tb, B_W), BF16), kv_scratch, kv_scratch, kv_scratch, kv_scratch,
               pltpu.VMEM((tb, B_W), F32), pltpu.VMEM((tb, B_W), F32)]
    return pl.pallas_call(
        functools.partial(_swa_kernel, layer),
        out_shape=jax.ShapeDtypeStruct((b, t, d), F32),
        grid=(b, t // tb),
        in_specs=in_specs,
        out_specs=pl.BlockSpec((1, tb, d), tile),
        scratch_shapes=scratch,
        compiler_params=pltpu.CompilerParams(
            dimension_semantics=("arbitrary", "arbitrary"), vmem_limit_bytes=VMEM_LIMIT),
        name="swa_layer",
    )(sinks, x, norm.reshape(n, 1, d), w_in, q_gain_tiled, k, k, v, v, bias, w_out)


def kernel(x, a_norm, a_w_in, a_conv, a_A_log, a_dt_bias, a_o_gain, a_w_out, kv_norm, w_kv, k_gain, rel_bias, b_norm, b_w_in, b_q_gain, b_sinks, b_w_out):
    n_a = a_w_in.shape[0]
    n_b = b_w_in.shape[0]
    a_w_in_b = a_w_in[:, :, :A_CONV_CH + A_V_W].astype(BF16)
    a_w_ab = a_w_in[:, :, A_CONV_CH + A_V_W:].astype(BF16)
    a_w_ab_t = jnp.swapaxes(a_w_ab, 1, 2)
    a_w_out_b = a_w_out.astype(BF16)
    b_w_in_b = b_w_in.astype(BF16)
    b_w_out_b = b_w_out.astype(BF16)
    q_gain_tiled = jnp.tile(b_q_gain, (1, B_Q_HEADS)).reshape(n_b, 1, B_W)
    for i in range(n_a):
        x = _deltanet_layer(x, i, a_norm, a_w_in_b, a_w_ab, a_w_ab_t, a_conv, a_A_log, a_dt_bias,
                            a_o_gain, a_w_out_b)
    k, v = _shared_kv(x, kv_norm, w_kv, k_gain)
    bias = _band_bias(rel_bias)
    for j in range(n_b):
        x = _swa_layer(x, j, b_norm, b_w_in_b, q_gain_tiled, b_sinks, b_w_out_b, k, v, bias)
    return x
```

```python
import functools
import math

import numpy as np
import jax
import jax.numpy as jnp
from jax import lax
from jax.experimental import pallas as pl
from jax.experimental.pallas import tpu as pltpu

F32 = jnp.float32
BF16 = jnp.bfloat16

D_MODEL = 1024
EPS = 1e-6

A_HEADS = 8
A_DK = 128
A_DV = 256
A_QK_W = A_HEADS * A_DK
A_V_W = A_HEADS * A_DV
A_CONV_CH = 2 * A_QK_W + A_V_W
A_IN_W = A_CONV_CH + A_V_W + 2 * A_HEADS
CONV_W = 4
CHUNK = 64
A_TILE = 256
CONV_PAD = 8
PROJ_SLAB = 512
SBS_HEADS = 4

B_Q_HEADS = 32
B_KV_HEADS = 4
B_GROUP = B_Q_HEADS // B_KV_HEADS
B_HD = 64
B_W = B_Q_HEADS * B_HD
B_KV_W = B_KV_HEADS * B_HD
WINDOW = 128
BLOCK = 128
B_TILE = 512
SWA_HEAD_GROUP = 8
KV_TILE = 512
N_BUCKETS = 32
MAX_DIST = 128
NEG = -1e30
LOG2E = math.log2(math.e)
GATE_SLAB = 256

LANES = 128
VMEM_LIMIT = 56 * 1024 * 1024

_NT = (((1,), (1,)), ((), ()))
_NN = (((1,), (0,)), ((), ()))
_TN = (((0,), (0,)), ((), ()))


def _mm(a, b, dims=_NN):
    return lax.dot_general(a.astype(BF16), b.astype(BF16), dims, preferred_element_type=F32)


def _split3(a):
    hi = a.astype(BF16)
    rest = a - hi.astype(F32)
    mid = rest.astype(BF16)
    lo = (rest - mid.astype(F32)).astype(BF16)
    return hi, mid, lo


def _silu(x):
    return x * jax.nn.sigmoid(x)


def _rms_rows(x, gain):
    return x * lax.rsqrt(jnp.mean(x * x, axis=-1, keepdims=True) + EPS) * gain


def _const_spec(shape):
    nd = len(shape)
    return pl.BlockSpec(shape, lambda *_: (0,) * nd, pipeline_mode=pl.Buffered(1))


def _layer_spec(shape, layer):
    nd = len(shape)
    return pl.BlockSpec((1,) + tuple(shape), lambda *_: (layer,) + (0,) * nd,
                        pipeline_mode=pl.Buffered(1))


def _deltanet_kernel(x_ref, norm_ref, win_ref, wab_ref, conv_ref,
                     alog_ref, dtb_ref, alog_c_ref, dtb_c_ref, ogain_ref, wout_ref,
                     out_ref, *scratch):
    n_slabs = A_CONV_CH // PROJ_SLAB
    ext_refs = scratch[:n_slabs]
    state_ref, q_ref, k_ref, v_ref, gate_ref, og_ref = scratch[n_slabs:]
    tm = A_TILE
    c = CHUNK
    sbs = SBS_HEADS
    n_chunks = tm // c
    log_c = int(math.log2(c))
    assert n_chunks == sbs
    first = pl.program_id(1) == 0

    @pl.when(first)
    def _():
        for ext_ref in ext_refs:
            ext_ref[0:CONV_PAD, :] = jnp.zeros((CONV_PAD, PROJ_SLAB), F32)
        state_ref[...] = jnp.zeros_like(state_ref)

    x = x_ref[0]
    hb = _rms_rows(x, norm_ref[0]).astype(BF16)
    z_col = A_CONV_CH


    def project(p):
        ext_refs[p][CONV_PAD:CONV_PAD + tm, :] = _mm(hb, win_ref[0, :, p * PROJ_SLAB:(p + 1) * PROJ_SLAB])

    def conv_slab(s):
        cols = slice(s * LANES, (s + 1) * LANES)
        p, off = divmod(s * LANES, PROJ_SLAB)
        ext = ext_refs[p][:, off:off + LANES]
        acc = ext * conv_ref[0, 0:1, cols]
        for j in range(1, CONV_W):
            acc = ext * conv_ref[0, j:j + 1, cols] + pltpu.roll(acc, 1, 0)
        y = _silu(acc[CONV_PAD:CONV_PAD + tm])
        if s < A_HEADS:
            y = y * (lax.rsqrt(jnp.sum(y * y, axis=-1, keepdims=True) + EPS) * (A_DK ** -0.5))
            q_ref[:, cols] = y
        elif s < 2 * A_HEADS:
            y = y * lax.rsqrt(jnp.sum(y * y, axis=-1, keepdims=True) + EPS)
            k_ref[:, s * LANES - A_QK_W:(s + 1) * LANES - A_QK_W] = y
        else:
            v_ref[:, s * LANES - 2 * A_QK_W:(s + 1) * LANES - 2 * A_QK_W] = y

    def gate_proj(h):
        wz = win_ref[0, :, z_col + h * A_DV:z_col + (h + 1) * A_DV]
        gate_ref[:, h * A_DV:(h + 1) * A_DV] = _silu(_mm(hb, wz))

    ab = _mm(hb, wab_ref[0])
    ab_t = ab.T

    qk_slabs = 2 * A_QK_W // PROJ_SLAB
    for p in range(qk_slabs // 2):
        project(p)

    beta_c = jax.nn.sigmoid(ab[:, 0:A_HEADS])
    g_c = -jnp.exp(alog_ref[0]) * jax.nn.softplus(ab[:, A_HEADS:2 * A_HEADS] + dtb_ref[0])
    g_r = -jnp.exp(alog_c_ref[0]) * jax.nn.softplus(ab_t[A_HEADS:2 * A_HEADS, :] + dtb_c_ref[0])

    ri = lax.broadcasted_iota(jnp.int32, (tm, tm), 0)
    ci = lax.broadcasted_iota(jnp.int32, (tm, tm), 1)
    same = (ri >> log_c) == (ci >> log_c)
    sums_c = jnp.concatenate([(same & (ri >= ci)).astype(BF16), same.astype(BF16)], axis=0)
    sums_c = sum(jnp.dot(sums_c, part, preferred_element_type=F32) for part in _split3(g_c))
    gcum_c = sums_c[0:tm]
    gtot_c = sums_c[tm:2 * tm]
    gcum_r = jnp.dot(jnp.concatenate(_split3(g_r), axis=0), (same & (ri <= ci)).astype(BF16),
                     preferred_element_type=F32)
    gcum_r = gcum_r[0:A_HEADS] + gcum_r[A_HEADS:2 * A_HEADS] + gcum_r[2 * A_HEADS:3 * A_HEADS]
    gcum_rot = [gcum_r] + [pltpu.roll(gcum_r, k * c, 1) for k in range(1, n_chunks)]

    for p in range(qk_slabs // 2, qk_slabs):
        project(p)
    for s in range(2 * A_HEADS):
        conv_slab(s)
    for p in range(qk_slabs, n_slabs):
        project(p)
    for h in range(A_HEADS // 2):
        gate_proj(h)
    for s in range(2 * A_HEADS, A_CONV_CH // LANES):
        conv_slab(s)
    for ext_ref in ext_refs:
        ext_ref[0:CONV_PAD, :] = ext_ref[tm:tm + CONV_PAD, :]

    sw = sbs * c
    row_i = lax.broadcasted_iota(jnp.int32, (c, sw), 0)
    lane_i = lax.broadcasted_iota(jnp.int32, (c, sw), 1)
    pos_i = lane_i & (c - 1)
    blk_i = lane_i >> log_c
    incl = row_i >= pos_i
    strict = row_i > pos_i
    eye = (row_i == pos_i).astype(F32)
    bd_p = ((lax.broadcasted_iota(jnp.int32, (sw, sw), 0) >> log_c)
            == (lax.broadcasted_iota(jnp.int32, (sw, sw), 1) >> log_c))
    bd_k = ((lax.broadcasted_iota(jnp.int32, (sw, sbs * A_DK), 0) >> log_c)
            == (lax.broadcasted_iota(jnp.int32, (sw, sbs * A_DK), 1) >> int(math.log2(A_DK))))

    def spread(cols):
        out = cols[:, sbs - 1:sbs]
        for hp in range(sbs - 2, -1, -1):
            out = jnp.where(blk_i == hp, cols[:, hp:hp + 1], out)
        return out

    def block_diag(m, mask):
        return jnp.where(mask, jnp.concatenate([m] * sbs, axis=0), jnp.zeros((), m.dtype))

    units =[(g, j) for g in range(A_HEADS // sbs) for j in range(n_chunks)]
    a_qk, t_inv, pw = {}, {}, {}
    for g, j in units:
        rows = slice(j * c, (j + 1) * c)
        hcols = slice(g * sbs * A_DK, (g + 1) * sbs * A_DK)
        qg = q_ref[rows, hcols].astype(BF16)
        kg = k_ref[rows, hcols].astype(BF16)
        qk_kk = lax.dot_general(jnp.concatenate([qg, kg], axis=0), block_diag(kg, bd_k), _NT,
                                preferred_element_type=F32)
        gcol = spread(gcum_c[rows, g * sbs:(g + 1) * sbs])
        grow = gcum_rot[(sbs - 1 - j) % n_chunks][g * sbs + sbs - 1:g * sbs + sbs, :]
        for hp in range(sbs - 2, -1, -1):
            grow = jnp.where(blk_i[0:1] == hp,
                             gcum_rot[(hp - j) % n_chunks][g * sbs + hp:g * sbs + hp + 1, :], grow)
        decay = jnp.where(incl, jnp.exp(jnp.minimum(gcol - grow, 0.0)), 0.0)
        a_qk[g, j] = qk_kk[0:c] * decay
        low = jnp.where(strict, qk_kk[c:2 * c] * decay * spread(beta_c[rows, g * sbs:(g + 1) * sbs]), 0.0)
        t_inv[g, j] = eye - low
        pw[g, j] = (-low).astype(BF16)

    for rnd in range(log_c):
        if 1 <= rnd <= A_HEADS // 2:
            gate_proj(A_HEADS // 2 + rnd - 1)
        for u in units:
            bd = block_diag(pw[u], bd_p)
            if rnd == 0:
                pw[u] = jnp.dot(pw[u], bd, preferred_element_type=F32).astype(BF16)
            elif rnd == log_c - 1:
                t_inv[u] = t_inv[u] + jnp.dot(t_inv[u].astype(BF16), bd, preferred_element_type=F32)
            else:
                res = jnp.dot(jnp.concatenate([pw[u], t_inv[u].astype(BF16)], axis=0), bd,
                              preferred_element_type=F32)
                pw[u] = res[0:c].astype(BF16)
                t_inv[u] = t_inv[u] + res[c:2 * c]

    uw, a_h, q_dec, k_dec = {}, {}, {}, {}
    for h in range(A_HEADS):
        g, hp = divmod(h, sbs)
        qh = q_ref[:, h * A_DK:(h + 1) * A_DK]
        kh = k_ref[:, h * A_DK:(h + 1) * A_DK]
        vh = v_ref[:, h * A_DV:(h + 1) * A_DV]
        gc = gcum_c[:, h:h + 1]
        bc = beta_c[:, h:h + 1]
        eg = jnp.exp(gc)
        rhs = jnp.concatenate([(vh * bc).astype(BF16), (kh * (bc * eg)).astype(BF16)], axis=1)
        q_dec[h] = (qh * eg).astype(BF16)
        k_dec[h] = (kh * jnp.exp(gtot_c[:, h:h + 1] - gc)).astype(BF16)
        for j in range(n_chunks):
            rows = slice(j * c, (j + 1) * c)
            t_hc = t_inv[g, j][:, hp * c:(hp + 1) * c].astype(BF16)
            uw[h, j] = jnp.dot(t_hc, rhs[rows], preferred_element_type=F32)
            a_h[h, j] = a_qk[g, j][:, hp * c:(hp + 1) * c].astype(BF16)

    st = {h: state_ref[h] for h in range(A_HEADS)}
    o_parts = {h: [] for h in range(A_HEADS)}
    for j in range(n_chunks):
        rows = slice(j * c, (j + 1) * c)
        wq = {}
        for h in range(A_HEADS):
            w = uw[h, j][:, A_DV:].astype(BF16)
            wq[h] = jnp.dot(jnp.concatenate([w, q_dec[h][rows]], axis=0), st[h].astype(BF16),
                            preferred_element_type=F32)
        for h in range(A_HEADS):
            v_new = (uw[h, j][:, 0:A_DV] - wq[h][0:c]).astype(BF16)
            o_parts[h].append(wq[h][c:2 * c] + jnp.dot(a_h[h, j], v_new, preferred_element_type=F32))
            g_last = jnp.exp(gtot_c[j * c:j * c + 1, h:h + 1])
            st[h] = st[h] * g_last + lax.dot_general(k_dec[h][rows], v_new, _TN,
                                                     preferred_element_type=F32)

    for h in range(A_HEADS):
        state_ref[h] = st[h]
        o = _rms_rows(jnp.concatenate(o_parts[h], axis=0), ogain_ref[0])
        og_ref[:, h * A_DV:(h + 1) * A_DV] = (o * gate_ref[:, h * A_DV:(h + 1) * A_DV]).astype(BF16)

    out_ref[0] = x + jnp.dot(og_ref[...], wout_ref[0], preferred_element_type=F32)


def _deltanet_layer(x, layer, norm, w_in, w_ab, conv_w, a_log, dt_bias, o_gain, w_out):
    b, t, d = x.shape
    tm = A_TILE
    n = norm.shape[0]
    in_specs = [
        pl.BlockSpec((1, tm, d), lambda bi, ti: (bi, ti, 0)),
        _layer_spec((1, d), layer),
        _layer_spec((d, A_IN_W), layer),
        _layer_spec((d, LANES), layer),
        _layer_spec((CONV_W, A_CONV_CH), layer),
        _layer_spec((1, A_HEADS), layer),
        _layer_spec((1, A_HEADS), layer),
        _layer_spec((A_HEADS, 1), layer),
        _layer_spec((A_HEADS, 1), layer),
        _layer_spec((1, A_DV), layer),
        _layer_spec((A_V_W, d), layer),
    ]
    scratch = [pltpu.VMEM((tm + CONV_PAD, PROJ_SLAB), F32)
               for _ in range(A_CONV_CH // PROJ_SLAB)] + [
        pltpu.VMEM((A_HEADS, A_DK, A_DV), F32),
        pltpu.VMEM((tm, A_QK_W), F32),
        pltpu.VMEM((tm, A_QK_W), F32),
        pltpu.VMEM((tm, A_V_W), F32),
        pltpu.VMEM((tm, A_V_W), F32),
        pltpu.VMEM((tm, A_V_W), BF16),
    ]
    return pl.pallas_call(
        _deltanet_kernel,
        out_shape=jax.ShapeDtypeStruct((b, t, d), F32),
        grid=(b, t // tm),
        in_specs=in_specs,
        out_specs=pl.BlockSpec((1, tm, d), lambda bi, ti: (bi, ti, 0)),
        scratch_shapes=scratch,
        compiler_params=pltpu.CompilerParams(
            dimension_semantics=("arbitrary", "arbitrary"), vmem_limit_bytes=VMEM_LIMIT),
        name="deltanet_layer",
    )(x, norm.reshape(n, 1, d), w_in, w_ab, conv_w,
      a_log.reshape(n, 1, A_HEADS), dt_bias.reshape(n, 1, A_HEADS),
      a_log.reshape(n, A_HEADS, 1), dt_bias.reshape(n, A_HEADS, 1),
      o_gain.reshape(n, 1, A_DV), w_out)


def _head_sum_matrix(width):
    r = lax.broadcasted_iota(jnp.int32, (width, width), 0) // B_HD
    c = lax.broadcasted_iota(jnp.int32, (width, width), 1) // B_HD
    return (r == c).astype(BF16)


def _head_rms(x, gain_tiled):
    width = x.shape[-1]
    summer = _head_sum_matrix(LANES)
    sq = x * x
    hi = sq.astype(BF16)
    lo = (sq - hi.astype(F32)).astype(BF16)
    parts = []
    for s in range(width // LANES):
        cols = slice(s * LANES, (s + 1) * LANES)
        parts.append(jnp.dot(hi[:, cols], summer, preferred_element_type=F32)
                     + jnp.dot(lo[:, cols], summer, preferred_element_type=F32))
    ss = jnp.concatenate(parts, axis=-1) if len(parts) > 1 else parts[0]
    return x * lax.rsqrt(ss * (1.0 / B_HD) + EPS) * gain_tiled


def _shared_kv_kernel(x_ref, norm_ref, wkv_ref, kgain_ref, k_ref, v_ref):
    hb = _rms_rows(x_ref[0], norm_ref[...]).astype(BF16)
    kv = jnp.dot(hb, wkv_ref[...], preferred_element_type=F32)
    k_ref[0] = _head_rms(kv[:, :B_KV_W], kgain_ref[...])
    v_ref[0] = kv[:, B_KV_W:]


def _shared_kv(x, kv_norm, w_kv, k_gain):
    b, t, d = x.shape
    tk = min(KV_TILE, t)
    out = jax.ShapeDtypeStruct((b, t, B_KV_W), F32)
    spec_out = pl.BlockSpec((1, tk, B_KV_W), lambda bi, ti: (bi, ti, 0))
    return pl.pallas_call(
        _shared_kv_kernel,
        out_shape=(out, out),
        grid=(b, t // tk),
        in_specs=[pl.BlockSpec((1, tk, d), lambda bi, ti: (bi, ti, 0)),
                  _const_spec((1, d)), _const_spec((d, 2 * B_KV_W)), _const_spec((1, B_KV_W))],
        out_specs=(spec_out, spec_out),
        compiler_params=pltpu.CompilerParams(
            dimension_semantics=("arbitrary", "arbitrary"), vmem_limit_bytes=VMEM_LIMIT),
        name="shared_kv",
    )(x, kv_norm.reshape(1, d), w_kv.astype(BF16), jnp.tile(k_gain, B_KV_HEADS).reshape(1, B_KV_W))


def _bucket_ranges():
    dist = np.arange(WINDOW)
    max_exact = N_BUCKETS // 2
    large = max_exact + (np.log(np.maximum(dist, 1) / max_exact) / np.log(MAX_DIST / max_exact)
                         * (N_BUCKETS - max_exact)).astype(np.int64)
    large = np.minimum(large, N_BUCKETS - 1)
    bucket = np.where(dist < max_exact, dist, large)
    ranges = []
    for bkt in range(N_BUCKETS):
        idx = np.nonzero(bucket == bkt)[0]
        if idx.size:
            assert np.all(np.diff(idx) == 1)
            ranges.append((bkt, int(idx[0]), int(idx[-1])))
    return ranges


def _band_bias_kernel(rel_ref, out_ref):
    h = pl.program_id(0)
    qi = lax.broadcasted_iota(jnp.int32, (BLOCK, BLOCK), 0)
    si = lax.broadcasted_iota(jnp.int32, (BLOCK, BLOCK), 1)
    dist = jnp.where(si > qi, qi - si + BLOCK, qi - si)
    ranges = _bucket_ranges()
    assert ranges[0][1] == 0 and ranges[-1][2] == WINDOW - 1
    acc = jnp.zeros((BLOCK, BLOCK), F32)
    for bkt, lo, hi in ranges:
        acc = jnp.where((dist >= lo) & (dist <= hi), rel_ref[bkt, h] * LOG2E, acc)
    out_ref[0] = acc


def _band_bias(rel_bias):
    return pl.pallas_call(
        _band_bias_kernel,
        out_shape=jax.ShapeDtypeStruct((B_Q_HEADS, BLOCK, BLOCK), F32),
        grid=(B_Q_HEADS,),
        in_specs=[pl.BlockSpec(memory_space=pltpu.SMEM)],
        out_specs=pl.BlockSpec((1, BLOCK, BLOCK), lambda h: (h, 0, 0)),
        compiler_params=pltpu.CompilerParams(dimension_semantics=("arbitrary",)),
        name="band_bias",
    )(rel_bias)


def _swa_kernel(layer, sinks_ref, x_ref, norm_ref, win_ref, qgain_ref,
                kprev_ref, kcur_ref, vprev_ref, vcur_ref, bias_ref, wout_ref,
                out_ref,
                q_ref, ka_ref, kb_ref, va_ref, vb_ref, o_ref, gate_ref):
    tb = B_TILE
    first = pl.program_id(1) == 0
    x = x_ref[0]
    hb = _rms_rows(x, norm_ref[0]).astype(BF16)
    sinks = [sinks_ref[layer, hq] * LOG2E for hq in range(B_Q_HEADS)]
    group_w = SWA_HEAD_GROUP * B_HD

    def q_proj(s):
        cols = slice(s * group_w, (s + 1) * group_w)
        q = jnp.dot(hb, win_ref[0, :, cols], preferred_element_type=F32)
        q_ref[:, cols] = _head_rms(q, qgain_ref[0, :, cols] * (B_HD ** -0.5 * LOG2E)).astype(BF16)

    def gate_proj(p):
        wz = win_ref[0, :, B_W + p * GATE_SLAB:B_W + (p + 1) * GATE_SLAB]
        gate_ref[:, p * GATE_SLAB:(p + 1) * GATE_SLAB] = _silu(jnp.dot(hb, wz, preferred_element_type=F32))

    lane = lax.broadcasted_iota(jnp.int32, (BLOCK + tb, LANES), 1)
    low_half = lane < B_HD
    for t2 in range(B_KV_W // LANES):
        cols = slice(t2 * LANES, (t2 + 1) * LANES)
        for src_prev, src_cur, dst_a, dst_b in ((kprev_ref, kcur_ref, ka_ref, kb_ref),
                                                (vprev_ref, vcur_ref, va_ref, vb_ref)):
            full = jnp.concatenate([src_prev[0, :, cols], src_cur[0, :, cols]], axis=0)
            swapped = pltpu.roll(full, B_HD, 1)
            zero = jnp.zeros_like(full)
            dst_a[2 * t2] = jnp.where(low_half, full, zero).astype(BF16)
            dst_b[2 * t2] = jnp.where(low_half, zero, swapped).astype(BF16)
            dst_a[2 * t2 + 1] = jnp.where(low_half, swapped, zero).astype(BF16)
            dst_b[2 * t2 + 1] = jnp.where(low_half, zero, full).astype(BF16)

    lane_q = lax.broadcasted_iota(jnp.int32, (BLOCK, LANES), 1)
    from_prev = (lax.broadcasted_iota(jnp.int32, (BLOCK, BLOCK), 1)
                 > lax.broadcasted_iota(jnp.int32, (BLOCK, BLOCK), 0))
    pen = jnp.where(first & from_prev, NEG, 0.0)

    groups = [(qb, h0) for qb in range(tb // BLOCK) for h0 in range(0, B_Q_HEADS, SWA_HEAD_GROUP)]

    def qk_logits(qb, h0):
        rows = slice(qb * BLOCK, (qb + 1) * BLOCK)
        band = slice(qb * BLOCK, qb * BLOCK + 2 * BLOCK)
        j = h0 // B_GROUP
        kz = (ka_ref[j, band, :], kb_ref[j, band, :])
        out = {}
        for hq in range(h0, h0 + SWA_HEAD_GROUP):
            qpair = q_ref[rows, (hq // 2) * LANES:(hq // 2 + 1) * LANES]
            lg2 = lax.dot_general(qpair, kz[hq % 2], _NT, preferred_element_type=F32)
            lg = jnp.where(from_prev, lg2[:, 0:BLOCK], lg2[:, BLOCK:2 * BLOCK]) + bias_ref[hq]
            out[hq] = lg + pen if qb == 0 else lg
        return out

    def softmax_pv(qb, h0, logits):
        rows = slice(qb * BLOCK, (qb + 1) * BLOCK)
        band = slice(qb * BLOCK, qb * BLOCK + 2 * BLOCK)
        j = h0 // B_GROUP
        vz = (va_ref[j, band, :], vb_ref[j, band, :])
        heads = range(h0, h0 + SWA_HEAD_GROUP)
        m, pexp, inv = {}, {}, {}
        for hq in heads:
            m[hq] = jnp.maximum(jnp.max(logits[hq], axis=-1, keepdims=True), sinks[hq])
        for hq in heads:
            pe = jnp.exp2(logits[hq] - m[hq])
            denom = jnp.sum(pe, axis=-1, keepdims=True) + jnp.exp2(sinks[hq] - m[hq])
            inv[hq] = 1.0 / denom
            pexp[hq] = jnp.concatenate([jnp.where(from_prev, pe, 0.0).astype(BF16),
                                        jnp.where(from_prev, 0.0, pe).astype(BF16)], axis=1)
        for hq in heads[::2]:
            pv = (jnp.dot(pexp[hq], vz[0], preferred_element_type=F32)
                  + jnp.dot(pexp[hq + 1], vz[1], preferred_element_type=F32))
            p = hq // 2
            o_ref[rows, p * LANES:(p + 1) * LANES] = pv * jnp.where(lane_q < B_HD, inv[hq], inv[hq + 1])

    n_gate = B_W // GATE_SLAB
    groups_per_block = B_Q_HEADS // SWA_HEAD_GROUP
    q_proj(0)
    logits = qk_logits(*groups[0])
    for gi, grp in enumerate(groups):
        if gi + 1 < groups_per_block:
            q_proj(gi + 1)
        nxt = qk_logits(*groups[gi + 1]) if gi + 1 < len(groups) else None
        softmax_pv(*grp, logits)
        for p in range(gi * n_gate // groups_per_block, min(n_gate, (gi + 1) * n_gate // groups_per_block)):
            gate_proj(p)
        if (gi + 1) % groups_per_block == 0:
            rows = slice(grp[0] * BLOCK, (grp[0] + 1) * BLOCK)
            og = (o_ref[rows, :] * gate_ref[rows, :]).astype(BF16)
            out_ref[0, rows, :] = x_ref[0, rows, :] + jnp.dot(og, wout_ref[0], preferred_element_type=F32)
        logits = nxt


def _swa_layer(x, layer, norm, w_in, q_gain_tiled, sinks, w_out, k, v, bias):
    b, t, d = x.shape
    tb = B_TILE
    nblk = tb // BLOCK
    n = norm.shape[0]
    tile = lambda bi, ti: (bi, ti, 0)
    prev = lambda bi, ti: (bi, jnp.maximum(ti * nblk - 1, 0), 0)
    in_specs = [
        pl.BlockSpec(memory_space=pltpu.SMEM),
        pl.BlockSpec((1, tb, d), tile),
        _layer_spec((1, d), layer),
        _layer_spec((d, 2 * B_W), layer),
        _layer_spec((1, B_W), layer),
        pl.BlockSpec((1, BLOCK, B_KV_W), prev),
        pl.BlockSpec((1, tb, B_KV_W), tile),
        pl.BlockSpec((1, BLOCK, B_KV_W), prev),
        pl.BlockSpec((1, tb, B_KV_W), tile),
        _const_spec((B_Q_HEADS, BLOCK, BLOCK)),
        _layer_spec((B_W, d), layer),
    ]
    kv_scratch = pltpu.VMEM((B_KV_HEADS, BLOCK + tb, LANES), BF16)
    scratch = [pltpu.VMEM((tb, B_W), BF16), kv_scratch, kv_scratch, kv_scratch, kv_scratch,
               pltpu.VMEM((tb, B_W), F32), pltpu.VMEM((tb, B_W), F32)]
    return pl.pallas_call(
        functools.partial(_swa_kernel, layer),
        out_shape=jax.ShapeDtypeStruct((b, t, d), F32),
        grid=(b, t // tb),
        in_specs=in_specs,
        out_specs=pl.BlockSpec((1, tb, d), tile),
        scratch_shapes=scratch,
        compiler_params=pltpu.CompilerParams(
            dimension_semantics=("arbitrary", "arbitrary"), vmem_limit_bytes=VMEM_LIMIT),
        name="swa_layer",
    )(sinks, x, norm.reshape(n, 1, d), w_in, q_gain_tiled, k, k, v, v, bias, w_out)


def kernel(x, a_norm, a_w_in, a_conv, a_A_log, a_dt_bias, a_o_gain, a_w_out, kv_norm, w_kv, k_gain, rel_bias, b_norm, b_w_in, b_q_gain, b_sinks, b_w_out):
    n_a = a_w_in.shape[0]
    n_b = b_w_in.shape[0]
    a_w_in_b = a_w_in.astype(BF16)
    a_w_ab = jnp.pad(a_w_in[:, :, A_CONV_CH + A_V_W:],
                     ((0, 0), (0, 0), (0, LANES - 2 * A_HEADS))).astype(BF16)
    a_w_out_b = a_w_out.astype(BF16)
    b_w_in_b = b_w_in.astype(BF16)
    b_w_out_b = b_w_out.astype(BF16)
    q_gain_tiled = jnp.tile(b_q_gain, (1, B_Q_HEADS)).reshape(n_b, 1, B_W)
    for i in range(n_a):
        x = _deltanet_layer(x, i, a_norm, a_w_in_b, a_w_ab, a_conv, a_A_log, a_dt_bias,
                            a_o_gain, a_w_out_b)
    k, v = _shared_kv(x, kv_norm, w_kv, k_gain)
    bias = _band_bias(rel_bias)
    for j in range(n_b):
        x = _swa_layer(x, j, b_norm, b_w_in_b, q_gain_tiled, b_sinks, b_w_out_b, k, v, bias)
    return x
```

```python
import functools
import math

import numpy as np
import jax
import jax.numpy as jnp
from jax import lax
from jax.experimental import pallas as pl
from jax.experimental.pallas import tpu as pltpu

F32 = jnp.float32
BF16 = jnp.bfloat16

D_MODEL = 1024
EPS = 1e-6

A_HEADS = 8
A_DK = 128
A_DV = 256
A_QK_W = A_HEADS * A_DK
A_V_W = A_HEADS * A_DV
A_CONV_CH = 2 * A_QK_W + A_V_W
A_IN_W = A_CONV_CH + A_V_W + 2 * A_HEADS
CONV_W = 4
CHUNK = 64
A_TILE = 256
CONV_PAD = 8
PROJ_SLAB = 512
SBS_HEADS = 4

B_Q_HEADS = 32
B_KV_HEADS = 4
B_GROUP = B_Q_HEADS // B_KV_HEADS
B_HD = 64
B_W = B_Q_HEADS * B_HD
B_KV_W = B_KV_HEADS * B_HD
WINDOW = 128
BLOCK = 128
B_TILE = 512
SWA_HEAD_GROUP = 8
KV_TILE = 512
N_BUCKETS = 32
MAX_DIST = 128
NEG = -1e30
LOG2E = math.log2(math.e)
GATE_SLAB = 256
BIAS_HEADS_PER_STEP = 8

LANES = 128
VMEM_LIMIT = 56 * 1024 * 1024

_NT = (((1,), (1,)), ((), ()))
_NN = (((1,), (0,)), ((), ()))
_TN = (((0,), (0,)), ((), ()))


def _mm(a, b, dims=_NN):
    return lax.dot_general(a.astype(BF16), b.astype(BF16), dims, preferred_element_type=F32)


def _split3(a):
    hi = a.astype(BF16)
    rest = a - hi.astype(F32)
    mid = rest.astype(BF16)
    lo = (rest - mid.astype(F32)).astype(BF16)
    return hi, mid, lo


def _silu(x):
    h = 0.5 * x
    return h + h * jnp.tanh(h)


def _rms_rows(x, gain):
    return x * lax.rsqrt(jnp.mean(x * x, axis=-1, keepdims=True) + EPS) * gain


def _const_spec(shape):
    nd = len(shape)
    return pl.BlockSpec(shape, lambda *_: (0,) * nd, pipeline_mode=pl.Buffered(1))


def _layer_spec(shape, layer):
    nd = len(shape)
    return pl.BlockSpec((1,) + tuple(shape), lambda *_: (layer,) + (0,) * nd,
                        pipeline_mode=pl.Buffered(1))


def _deltanet_kernel(x_ref, norm_ref, win_ref, wab_ref, conv_ref,
                     alog_ref, dtb_ref, alog_c_ref, dtb_c_ref, ogain_ref, wout_ref,
                     out_ref, *scratch):
    n_slabs = A_CONV_CH // PROJ_SLAB
    ext_refs = scratch[:n_slabs]
    state_ref, q_ref, k_ref, v_ref, gate_ref, og_ref = scratch[n_slabs:]
    tm = A_TILE
    c = CHUNK
    sbs = SBS_HEADS
    n_chunks = tm // c
    log_c = int(math.log2(c))
    assert n_chunks == sbs
    first = pl.program_id(1) == 0

    @pl.when(first)
    def _():
        for ext_ref in ext_refs:
            ext_ref[0:CONV_PAD, :] = jnp.zeros((CONV_PAD, PROJ_SLAB), F32)
        state_ref[...] = jnp.zeros_like(state_ref)

    x = x_ref[0]
    hb = _rms_rows(x, norm_ref[0]).astype(BF16)
    z_col = A_CONV_CH


    def project(p):
        ext_refs[p][CONV_PAD:CONV_PAD + tm, :] = _mm(hb, win_ref[0, :, p * PROJ_SLAB:(p + 1) * PROJ_SLAB])

    def conv_slab(s):
        cols = slice(s * LANES, (s + 1) * LANES)
        p, off = divmod(s * LANES, PROJ_SLAB)
        ext = ext_refs[p][:, off:off + LANES]
        acc = ext * conv_ref[0, 0:1, cols]
        for j in range(1, CONV_W):
            acc = ext * conv_ref[0, j:j + 1, cols] + pltpu.roll(acc, 1, 0)
        y = _silu(acc[CONV_PAD:CONV_PAD + tm])
        if s < A_HEADS:
            y = y * (lax.rsqrt(jnp.sum(y * y, axis=-1, keepdims=True) + EPS) * (A_DK ** -0.5))
            q_ref[:, cols] = y
        elif s < 2 * A_HEADS:
            y = y * lax.rsqrt(jnp.sum(y * y, axis=-1, keepdims=True) + EPS)
            k_ref[:, s * LANES - A_QK_W:(s + 1) * LANES - A_QK_W] = y
        else:
            v_ref[:, s * LANES - 2 * A_QK_W:(s + 1) * LANES - 2 * A_QK_W] = y

    def gate_proj(h):
        wz = win_ref[0, :, z_col + h * A_DV:z_col + (h + 1) * A_DV]
        gate_ref[:, h * A_DV:(h + 1) * A_DV] = _silu(_mm(hb, wz))

    ab = _mm(hb, wab_ref[0])
    ab_t = ab.T

    qk_slabs = 2 * A_QK_W // PROJ_SLAB
    for p in range(qk_slabs // 2):
        project(p)

    beta_c = jax.nn.sigmoid(ab[:, 0:A_HEADS])
    g_c = -jnp.exp(alog_ref[0]) * jax.nn.softplus(ab[:, A_HEADS:2 * A_HEADS] + dtb_ref[0])
    g_r = -jnp.exp(alog_c_ref[0]) * jax.nn.softplus(ab_t[A_HEADS:2 * A_HEADS, :] + dtb_c_ref[0])

    ri = lax.broadcasted_iota(jnp.int32, (tm, tm), 0)
    ci = lax.broadcasted_iota(jnp.int32, (tm, tm), 1)
    same = (ri >> log_c) == (ci >> log_c)
    sums_c = jnp.concatenate([(same & (ri >= ci)).astype(BF16), same.astype(BF16)], axis=0)
    sums_c = sum(jnp.dot(sums_c, part, preferred_element_type=F32) for part in _split3(g_c))
    gcum_c = sums_c[0:tm]
    gtot_c = sums_c[tm:2 * tm]
    gcum_r = jnp.dot(jnp.concatenate(_split3(g_r), axis=0), (same & (ri <= ci)).astype(BF16),
                     preferred_element_type=F32)
    gcum_r = gcum_r[0:A_HEADS] + gcum_r[A_HEADS:2 * A_HEADS] + gcum_r[2 * A_HEADS:3 * A_HEADS]
    gcum_rot = [gcum_r] + [pltpu.roll(gcum_r, k * c, 1) for k in range(1, n_chunks)]

    for p in range(qk_slabs // 2, qk_slabs):
        project(p)
    for s in range(2 * A_HEADS):
        conv_slab(s)
    for p in range(qk_slabs, n_slabs):
        project(p)
    for h in range(A_HEADS // 2):
        gate_proj(h)
    for s in range(2 * A_HEADS, A_CONV_CH // LANES):
        conv_slab(s)
    for ext_ref in ext_refs:
        ext_ref[0:CONV_PAD, :] = ext_ref[tm:tm + CONV_PAD, :]

    sw = sbs * c
    row_i = lax.broadcasted_iota(jnp.int32, (c, sw), 0)
    lane_i = lax.broadcasted_iota(jnp.int32, (c, sw), 1)
    pos_i = lane_i & (c - 1)
    blk_i = lane_i >> log_c
    incl = row_i >= pos_i
    strict = row_i > pos_i
    eye = (row_i == pos_i).astype(F32)
    bd_p = ((lax.broadcasted_iota(jnp.int32, (sw, sw), 0) >> log_c)
            == (lax.broadcasted_iota(jnp.int32, (sw, sw), 1) >> log_c))
    bd_k = ((lax.broadcasted_iota(jnp.int32, (sw, sbs * A_DK), 0) >> log_c)
            == (lax.broadcasted_iota(jnp.int32, (sw, sbs * A_DK), 1) >> int(math.log2(A_DK))))

    def spread(cols):
        out = cols[:, sbs - 1:sbs]
        for hp in range(sbs - 2, -1, -1):
            out = jnp.where(blk_i == hp, cols[:, hp:hp + 1], out)
        return out

    def block_diag(m, mask):
        return jnp.where(mask, jnp.concatenate([m] * sbs, axis=0), jnp.zeros((), m.dtype))

    units = [(g, j) for g in range(A_HEADS // sbs) for j in range(n_chunks)]
    a_qk, t_inv, pw = {}, {}, {}
    for g, j in units:
        rows = slice(j * c, (j + 1) * c)
        hcols = slice(g * sbs * A_DK, (g + 1) * sbs * A_DK)
        qg = q_ref[rows, hcols].astype(BF16)
        kg = k_ref[rows, hcols].astype(BF16)
        qk_kk = lax.dot_general(jnp.concatenate([qg, kg], axis=0), block_diag(kg, bd_k), _NT,
                                preferred_element_type=F32)
        gcol = spread(gcum_c[rows, g * sbs:(g + 1) * sbs])
        grow = gcum_rot[(sbs - 1 - j) % n_chunks][g * sbs + sbs - 1:g * sbs + sbs, :]
        for hp in range(sbs - 2, -1, -1):
            grow = jnp.where(blk_i[0:1] == hp,
                             gcum_rot[(hp - j) % n_chunks][g * sbs + hp:g * sbs + hp + 1, :], grow)
        decay = jnp.where(incl, jnp.exp(jnp.minimum(gcol - grow, 0.0)), 0.0)
        a_qk[g, j] = qk_kk[0:c] * decay
        low = jnp.where(strict, qk_kk[c:2 * c] * decay * spread(beta_c[rows, g * sbs:(g + 1) * sbs]), 0.0)
        t_inv[g, j] = eye - low
        pw[g, j] = (-low).astype(BF16)

    for rnd in range(log_c):
        if 1 <= rnd <= A_HEADS // 2:
            gate_proj(A_HEADS // 2 + rnd - 1)
        for u in units:
            bd = block_diag(pw[u], bd_p)
            if rnd == 0:
                pw[u] = jnp.dot(pw[u], bd, preferred_element_type=F32).astype(BF16)
            elif rnd == log_c - 1:
                t_inv[u] = t_inv[u] + jnp.dot(t_inv[u].astype(BF16), bd, preferred_element_type=F32)
            else:
                res = jnp.dot(jnp.concatenate([pw[u], t_inv[u].astype(BF16)], axis=0), bd,
                              preferred_element_type=F32)
                pw[u] = res[0:c].astype(BF16)
                t_inv[u] = t_inv[u] + res[c:2 * c]

    uw, a_h, q_dec, k_dec = {}, {}, {}, {}
    for h in range(A_HEADS):
        g, hp = divmod(h, sbs)
        qh = q_ref[:, h * A_DK:(h + 1) * A_DK]
        kh = k_ref[:, h * A_DK:(h + 1) * A_DK]
        vh = v_ref[:, h * A_DV:(h + 1) * A_DV]
        gc = gcum_c[:, h:h + 1]
        bc = beta_c[:, h:h + 1]
        eg = jnp.exp(gc)
        rhs = jnp.concatenate([(vh * bc).astype(BF16), (kh * (bc * eg)).astype(BF16)], axis=1)
        q_dec[h] = (qh * eg).astype(BF16)
        k_dec[h] = (kh * jnp.exp(gtot_c[:, h:h + 1] - gc)).astype(BF16)
        for j in range(n_chunks):
            rows = slice(j * c, (j + 1) * c)
            t_hc = t_inv[g, j][:, hp * c:(hp + 1) * c].astype(BF16)
            uw[h, j] = jnp.dot(t_hc, rhs[rows], preferred_element_type=F32)
            a_h[h, j] = a_qk[g, j][:, hp * c:(hp + 1) * c].astype(BF16)

    st = {h: state_ref[h] for h in range(A_HEADS)}
    o_parts = {h: [] for h in range(A_HEADS)}
    for j in range(n_chunks):
        rows = slice(j * c, (j + 1) * c)
        wq = {}
        for h in range(A_HEADS):
            w = uw[h, j][:, A_DV:].astype(BF16)
            wq[h] = jnp.dot(jnp.concatenate([w, q_dec[h][rows]], axis=0), st[h].astype(BF16),
                            preferred_element_type=F32)
        for h in range(A_HEADS):
            v_new = (uw[h, j][:, 0:A_DV] - wq[h][0:c]).astype(BF16)
            o_parts[h].append(wq[h][c:2 * c] + jnp.dot(a_h[h, j], v_new, preferred_element_type=F32))
            g_last = jnp.exp(gtot_c[j * c:j * c + 1, h:h + 1])
            st[h] = st[h] * g_last + lax.dot_general(k_dec[h][rows], v_new, _TN,
                                                     preferred_element_type=F32)

    for h in range(A_HEADS):
        state_ref[h] = st[h]
        o = _rms_rows(jnp.concatenate(o_parts[h], axis=0), ogain_ref[0])
        og_ref[:, h * A_DV:(h + 1) * A_DV] = (o * gate_ref[:, h * A_DV:(h + 1) * A_DV]).astype(BF16)

    out_ref[0] = x + jnp.dot(og_ref[...], wout_ref[0], preferred_element_type=F32)


def _deltanet_layer(x, layer, norm, w_in, w_ab, conv_w, a_log, dt_bias, o_gain, w_out):
    b, t, d = x.shape
    tm = A_TILE
    n = norm.shape[0]
    in_specs = [
        pl.BlockSpec((1, tm, d), lambda bi, ti: (bi, ti, 0)),
        _layer_spec((1, d), layer),
        _layer_spec((d, A_IN_W), layer),
        _layer_spec((d, LANES), layer),
        _layer_spec((CONV_W, A_CONV_CH), layer),
        _layer_spec((1, A_HEADS), layer),
        _layer_spec((1, A_HEADS), layer),
        _layer_spec((A_HEADS, 1), layer),
        _layer_spec((A_HEADS, 1), layer),
        _layer_spec((1, A_DV), layer),
        _layer_spec((A_V_W, d), layer),
    ]
    scratch = [pltpu.VMEM((tm + CONV_PAD, PROJ_SLAB), F32)
               for _ in range(A_CONV_CH // PROJ_SLAB)] + [
        pltpu.VMEM((A_HEADS, A_DK, A_DV), F32),
        pltpu.VMEM((tm, A_QK_W), F32),
        pltpu.VMEM((tm, A_QK_W), F32),
        pltpu.VMEM((tm, A_V_W), F32),
        pltpu.VMEM((tm, A_V_W), F32),
        pltpu.VMEM((tm, A_V_W), BF16),
    ]
    return pl.pallas_call(
        _deltanet_kernel,
        out_shape=jax.ShapeDtypeStruct((b, t, d), F32),
        grid=(b, t // tm),
        in_specs=in_specs,
        out_specs=pl.BlockSpec((1, tm, d), lambda bi, ti: (bi, ti, 0)),
        scratch_shapes=scratch,
        compiler_params=pltpu.CompilerParams(
            dimension_semantics=("arbitrary", "arbitrary"), vmem_limit_bytes=VMEM_LIMIT),
        name="deltanet_layer",
    )(x, norm.reshape(n, 1, d), w_in, w_ab, conv_w,
      a_log.reshape(n, 1, A_HEADS), dt_bias.reshape(n, 1, A_HEADS),
      a_log.reshape(n, A_HEADS, 1), dt_bias.reshape(n, A_HEADS, 1),
      o_gain.reshape(n, 1, A_DV), w_out)


def _head_sum_matrix(width):
    r = lax.broadcasted_iota(jnp.int32, (width, width), 0) // B_HD
    c = lax.broadcasted_iota(jnp.int32, (width, width), 1) // B_HD
    return (r == c).astype(BF16)


def _head_rms(x, gain_tiled):
    width = x.shape[-1]
    summer = _head_sum_matrix(LANES)
    sq = (x * x).astype(BF16)
    parts = []
    for s in range(width // LANES):
        cols = slice(s * LANES, (s + 1) * LANES)
        parts.append(jnp.dot(sq[:, cols], summer, preferred_element_type=F32))
    ss = jnp.concatenate(parts, axis=-1) if len(parts) > 1 else parts[0]
    return x * lax.rsqrt(ss * (1.0 / B_HD) + EPS) * gain_tiled


def _shared_kv_kernel(x_ref, norm_ref, wkv_ref, kgain_ref, k_ref, v_ref):
    hb = _rms_rows(x_ref[0], norm_ref[...]).astype(BF16)
    kv = jnp.dot(hb, wkv_ref[...], preferred_element_type=F32)
    k_ref[0] = _head_rms(kv[:, :B_KV_W], kgain_ref[...])
    v_ref[0] = kv[:, B_KV_W:]


def _shared_kv(x, kv_norm, w_kv, k_gain):
    b, t, d = x.shape
    tk = min(KV_TILE, t)
    out = jax.ShapeDtypeStruct((b, t, B_KV_W), F32)
    spec_out = pl.BlockSpec((1, tk, B_KV_W), lambda bi, ti: (bi, ti, 0))
    return pl.pallas_call(
        _shared_kv_kernel,
        out_shape=(out, out),
        grid=(b, t // tk),
        in_specs=[pl.BlockSpec((1, tk, d), lambda bi, ti: (bi, ti, 0)),
                  _const_spec((1, d)), _const_spec((d, 2 * B_KV_W)), _const_spec((1, B_KV_W))],
        out_specs=(spec_out, spec_out),
        compiler_params=pltpu.CompilerParams(
            dimension_semantics=("arbitrary", "arbitrary"), vmem_limit_bytes=VMEM_LIMIT),
        name="shared_kv",
    )(x, kv_norm.reshape(1, d), w_kv.astype(BF16), jnp.tile(k_gain, B_KV_HEADS).reshape(1, B_KV_W))


def _bucket_ranges():
    dist = np.arange(WINDOW)
    max_exact = N_BUCKETS // 2
    large = max_exact + (np.log(np.maximum(dist, 1) / max_exact) / np.log(MAX_DIST / max_exact)
                         * (N_BUCKETS - max_exact)).astype(np.int64)
    large = np.minimum(large, N_BUCKETS - 1)
    bucket = np.where(dist < max_exact, dist, large)
    ranges = []
    for bkt in range(N_BUCKETS):
        idx = np.nonzero(bucket == bkt)[0]
        if idx.size:
            assert np.all(np.diff(idx) == 1)
            ranges.append((bkt, int(idx[0]), int(idx[-1])))
    return ranges


def _band_bias_kernel(rel_ref, out_ref):
    h0 = pl.program_id(0) * BIAS_HEADS_PER_STEP
    qi = lax.broadcasted_iota(jnp.int32, (BLOCK, BLOCK), 0)
    si = lax.broadcasted_iota(jnp.int32, (BLOCK, BLOCK), 1)
    dist = jnp.where(si > qi, qi - si + BLOCK, qi - si)
    ranges = _bucket_ranges()
    assert ranges[0][1] == 0 and ranges[-1][2] == WINDOW - 1
    in_bucket = [(bkt, (dist >= lo) & (dist <= hi)) for bkt, lo, hi in ranges]
    for i in range(BIAS_HEADS_PER_STEP):
        acc = jnp.zeros((BLOCK, BLOCK), F32)
        for bkt, hit in in_bucket:
            acc = jnp.where(hit, rel_ref[bkt, h0 + i] * LOG2E, acc)
        out_ref[i] = acc


def _band_bias(rel_bias):
    return pl.pallas_call(
        _band_bias_kernel,
        out_shape=jax.ShapeDtypeStruct((B_Q_HEADS, BLOCK, BLOCK), F32),
        grid=(B_Q_HEADS // BIAS_HEADS_PER_STEP,),
        in_specs=[pl.BlockSpec(memory_space=pltpu.SMEM)],
        out_specs=pl.BlockSpec((BIAS_HEADS_PER_STEP, BLOCK, BLOCK), lambda h: (h, 0, 0)),
        compiler_params=pltpu.CompilerParams(dimension_semantics=("arbitrary",)),
        name="band_bias",
    )(rel_bias)


def _swa_kernel(layer, sinks_ref, x_ref, norm_ref, win_ref, qgain_ref,
                kprev_ref, kcur_ref, vprev_ref, vcur_ref, bias_ref, wout_ref,
                out_ref,
                q_ref, ka_ref, kb_ref, va_ref, vb_ref, o_ref, gate_ref):
    tb = B_TILE
    first = pl.program_id(1) == 0
    x = x_ref[0]
    hb = _rms_rows(x, norm_ref[0]).astype(BF16)
    sinks = [sinks_ref[layer, hq] * LOG2E for hq in range(B_Q_HEADS)]
    group_w = SWA_HEAD_GROUP * B_HD

    def q_proj(s):
        cols = slice(s * group_w, (s + 1) * group_w)
        q = jnp.dot(hb, win_ref[0, :, cols], preferred_element_type=F32)
        q_ref[:, cols] = _head_rms(q, qgain_ref[0, :, cols] * (B_HD ** -0.5 * LOG2E)).astype(BF16)

    def gate_proj(p):
        wz = win_ref[0, :, B_W + p * GATE_SLAB:B_W + (p + 1) * GATE_SLAB]
        gate_ref[:, p * GATE_SLAB:(p + 1) * GATE_SLAB] = _silu(jnp.dot(hb, wz, preferred_element_type=F32))

    lane = lax.broadcasted_iota(jnp.int32, (BLOCK + tb, LANES), 1)
    low_half = lane < B_HD
    for t2 in range(B_KV_W // LANES):
        cols = slice(t2 * LANES, (t2 + 1) * LANES)
        for src_prev, src_cur, dst_a, dst_b in ((kprev_ref, kcur_ref, ka_ref, kb_ref),
                                                (vprev_ref, vcur_ref, va_ref, vb_ref)):
            full = jnp.concatenate([src_prev[0, :, cols], src_cur[0, :, cols]], axis=0)
            swapped = pltpu.roll(full, B_HD, 1)
            zero = jnp.zeros_like(full)
            dst_a[2 * t2] = jnp.where(low_half, full, zero).astype(BF16)
            dst_b[2 * t2] = jnp.where(low_half, zero, swapped).astype(BF16)
            dst_a[2 * t2 + 1] = jnp.where(low_half, swapped, zero).astype(BF16)
            dst_b[2 * t2 + 1] = jnp.where(low_half, zero, full).astype(BF16)

    lane_q = lax.broadcasted_iota(jnp.int32, (BLOCK, LANES), 1)
    from_prev = (lax.broadcasted_iota(jnp.int32, (BLOCK, BLOCK), 1)
                 > lax.broadcasted_iota(jnp.int32, (BLOCK, BLOCK), 0))
    pen = jnp.where(first & from_prev, NEG, 0.0)

    groups = [(qb, h0) for qb in range(tb // BLOCK) for h0 in range(0, B_Q_HEADS, SWA_HEAD_GROUP)]

    def qk_logits(qb, h0):
        rows = slice(qb * BLOCK, (qb + 1) * BLOCK)
        band = slice(qb * BLOCK, qb * BLOCK + 2 * BLOCK)
        j = h0 // B_GROUP
        kz = (ka_ref[j, band, :], kb_ref[j, band, :])
        out = {}
        for hq in range(h0, h0 + SWA_HEAD_GROUP):
            qpair = q_ref[rows, (hq // 2) * LANES:(hq // 2 + 1) * LANES]
            lg2 = lax.dot_general(qpair, kz[hq % 2], _NT, preferred_element_type=F32)
            lg = jnp.where(from_prev, lg2[:, 0:BLOCK], lg2[:, BLOCK:2 * BLOCK]) + bias_ref[hq]
            out[hq] = lg + pen if qb == 0 else lg
        return out

    def softmax_pv(qb, h0, logits):
        rows = slice(qb * BLOCK, (qb + 1) * BLOCK)
        band = slice(qb * BLOCK, qb * BLOCK + 2 * BLOCK)
        j = h0 // B_GROUP
        vz = (va_ref[j, band, :], vb_ref[j, band, :])
        heads = range(h0, h0 + SWA_HEAD_GROUP)
        m, pexp, inv = {}, {}, {}
        for hq in heads:
            m[hq] = jnp.maximum(jnp.max(logits[hq], axis=-1, keepdims=True), sinks[hq])
        for hq in heads:
            pe = jnp.exp2(logits[hq] - m[hq])
            denom = jnp.sum(pe, axis=-1, keepdims=True) + jnp.exp2(sinks[hq] - m[hq])
            inv[hq] = 1.0 / denom
            pexp[hq] = jnp.concatenate([jnp.where(from_prev, pe, 0.0).astype(BF16),
                                        jnp.where(from_prev, 0.0, pe).astype(BF16)], axis=1)
        for hq in heads[::2]:
            pv = (jnp.dot(pexp[hq], vz[0], preferred_element_type=F32)
                  + jnp.dot(pexp[hq + 1], vz[1], preferred_element_type=F32))
            p = hq // 2
            o_ref[rows, p * LANES:(p + 1) * LANES] = pv * jnp.where(lane_q < B_HD, inv[hq], inv[hq + 1])

    n_gate = B_W // GATE_SLAB
    groups_per_block = B_Q_HEADS // SWA_HEAD_GROUP
    q_proj(0)
    logits = qk_logits(*groups[0])
    for gi, grp in enumerate(groups):
        if gi + 1 < groups_per_block:
            q_proj(gi + 1)
        nxt = qk_logits(*groups[gi + 1]) if gi + 1 < len(groups) else None
        softmax_pv(*grp, logits)
        for p in range(gi * n_gate // groups_per_block, min(n_gate, (gi + 1) * n_gate // groups_per_block)):
            gate_proj(p)
        if (gi + 1) % groups_per_block == 0:
            rows = slice(grp[0] * BLOCK, (grp[0] + 1) * BLOCK)
            og = (o_ref[rows, :] * gate_ref[rows, :]).astype(BF16)
            out_ref[0, rows, :] = x_ref[0, rows, :] + jnp.dot(og, wout_ref[0], preferred_element_type=F32)
        logits = nxt


def _swa_layer(x, layer, norm, w_in, q_gain_tiled, sinks, w_out, k, v, bias):
    b, t, d = x.shape
    tb = B_TILE
    nblk = tb // BLOCK
    n = norm.shape[0]
    tile = lambda bi, ti: (bi, ti, 0)
    prev = lambda bi, ti: (bi, jnp.maximum(ti * nblk - 1, 0), 0)
    in_specs = [
        pl.BlockSpec(memory_space=pltpu.SMEM),
        pl.BlockSpec((1, tb, d), tile),
        _layer_spec((1, d), layer),
        _layer_spec((d, 2 * B_W), layer),
        _layer_spec((1, B_W), layer),
        pl.BlockSpec((1, BLOCK, B_KV_W), prev),
        pl.BlockSpec((1, tb, B_KV_W), tile),
        pl.BlockSpec((1, BLOCK, B_KV_W), prev),
        pl.BlockSpec((1, tb, B_KV_W), tile),
        _const_spec((B_Q_HEADS, BLOCK, BLOCK)),
        _layer_spec((B_W, d), layer),
    ]
    kv_scratch = pltpu.VMEM((B_KV_HEADS, BLOCK + tb, LANES), BF16)
    scratch = [pltpu.VMEM((tb, B_W), BF16), kv_scratch, kv_scratch, kv_scratch, kv_scratch,
               pltpu.VMEM((tb, B_W), F32), pltpu.VMEM((tb, B_W), F32)]
    return pl.pallas_call(
        functools.partial(_swa_kernel, layer),
        out_shape=jax.ShapeDtypeStruct((b, t, d), F32),
        grid=(b, t // tb),
        in_specs=in_specs,
        out_specs=pl.BlockSpec((1, tb, d), tile),
        scratch_shapes=scratch,
        compiler_params=pltpu.CompilerParams(
            dimension_semantics=("arbitrary", "arbitrary"), vmem_limit_bytes=VMEM_LIMIT),
        name="swa_layer",
    )(sinks, x, norm.reshape(n, 1, d), w_in, q_gain_tiled, k, k, v, v, bias, w_out)


def kernel(x, a_norm, a_w_in, a_conv, a_A_log, a_dt_bias, a_o_gain, a_w_out, kv_norm, w_kv, k_gain, rel_bias, b_norm, b_w_in, b_q_gain, b_sinks, b_w_out):
    n_a = a_w_in.shape[0]
    n_b = b_w_in.shape[0]
    a_w_in_b = a_w_in.astype(BF16)
    a_w_ab = jnp.pad(a_w_in[:, :, A_CONV_CH + A_V_W:],
                     ((0, 0), (0, 0), (0, LANES - 2 * A_HEADS))).astype(BF16)
    a_w_out_b = a_w_out.astype(BF16)
    b_w_in_b = b_w_in.astype(BF16)
    b_w_out_b = b_w_out.astype(BF16)
    q_gain_tiled = jnp.tile(b_q_gain, (1, B_Q_HEADS)).reshape(n_b, 1, B_W)
    for i in range(n_a):
        x = _deltanet_layer(x, i, a_norm, a_w_in_b, a_w_ab, a_conv, a_A_log, a_dt_bias,
                            a_o_gain, a_w_out_b)
    k, v = _shared_kv(x, kv_norm, w_kv, k_gain)
    bias = _band_bias(rel_bias)
    for j in range(n_b):
        x = _swa_layer(x, j, b_norm, b_w_in_b, q_gain_tiled, b_sinks, b_w_out_b, k, v, bias)
    return x
```

```python
import functools
import math

import numpy as np
import jax
import jax.numpy as jnp
from jax import lax
from jax.experimental import pallas as pl
from jax.experimental.pallas import tpu as pltpu

F32 = jnp.float32
BF16 = jnp.bfloat16

D_MODEL = 1024
EPS = 1e-6

A_HEADS = 8
A_DK = 128
A_DV = 256
A_QK_W = A_HEADS * A_DK
A_V_W = A_HEADS * A_DV
A_CONV_CH = 2 * A_QK_W + A_V_W
A_IN_W = A_CONV_CH + A_V_W + 2 * A_HEADS
CONV_W = 4
CHUNK = 64
A_TILE = 256
CONV_PAD = 8
PROJ_SLAB = 512
SBS_HEADS = 4

B_Q_HEADS = 32
B_KV_HEADS = 4
B_GROUP = B_Q_HEADS // B_KV_HEADS
B_HD = 64
B_W = B_Q_HEADS * B_HD
B_KV_W = B_KV_HEADS * B_HD
WINDOW = 128
BLOCK = 128
B_TILE = 512
SWA_HEAD_GROUP = 8
KV_TILE = 1024
N_BUCKETS = 32
MAX_DIST = 128
NEG = -1e30
LOG2E = math.log2(math.e)
GATE_SLAB = 256
BIAS_HEADS_PER_STEP = 8

LANES = 128
VMEM_LIMIT = 56 * 1024 * 1024

_NT = (((1,), (1,)), ((), ()))
_NN = (((1,), (0,)), ((), ()))
_TN = (((0,), (0,)), ((), ()))


def _mm(a, b, dims=_NN):
    return lax.dot_general(a.astype(BF16), b.astype(BF16), dims, preferred_element_type=F32)


def _split3(a):
    hi = a.astype(BF16)
    rest = a - hi.astype(F32)
    mid = rest.astype(BF16)
    lo = (rest - mid.astype(F32)).astype(BF16)
    return hi, mid, lo


def _silu(x):
    h = 0.5 * x
    return h + h * jnp.tanh(h)


def _rms_rows(x, gain):
    return x * lax.rsqrt(jnp.mean(x * x, axis=-1, keepdims=True) + EPS) * gain


def _const_spec(shape):
    nd = len(shape)
    return pl.BlockSpec(shape, lambda *_: (0,) * nd, pipeline_mode=pl.Buffered(1))


def _layer_spec(shape, layer):
    nd = len(shape)
    return pl.BlockSpec((1,) + tuple(shape), lambda *_: (layer,) + (0,) * nd,
                        pipeline_mode=pl.Buffered(1))


def _deltanet_kernel(x_ref, norm_ref, win_ref, wab_ref, conv_ref,
                     alog_ref, dtb_ref, alog_c_ref, dtb_c_ref, ogain_ref, wout_ref,
                     out_ref, *scratch):
    n_slabs = A_CONV_CH // PROJ_SLAB
    ext_refs = scratch[:n_slabs]
    state_ref, q_ref, k_ref, v_ref, gate_ref, og_ref = scratch[n_slabs:]
    tm = A_TILE
    c = CHUNK
    sbs = SBS_HEADS
    n_chunks = tm // c
    log_c = int(math.log2(c))
    assert n_chunks == sbs
    first = pl.program_id(1) == 0

    @pl.when(first)
    def _():
        for ext_ref in ext_refs:
            ext_ref[0:CONV_PAD, :] = jnp.zeros((CONV_PAD, PROJ_SLAB), F32)
        state_ref[...] = jnp.zeros_like(state_ref)

    x = x_ref[0]
    hb = _rms_rows(x, norm_ref[0]).astype(BF16)
    z_col = A_CONV_CH


    def project(p):
        ext_refs[p][CONV_PAD:CONV_PAD + tm, :] = _mm(hb, win_ref[0, :, p * PROJ_SLAB:(p + 1) * PROJ_SLAB])

    def conv_slab(s):
        cols = slice(s * LANES, (s + 1) * LANES)
        p, off = divmod(s * LANES, PROJ_SLAB)
        ext = ext_refs[p][:, off:off + LANES]
        acc = ext * conv_ref[0, 0:1, cols]
        for j in range(1, CONV_W):
            acc = ext * conv_ref[0, j:j + 1, cols] + pltpu.roll(acc, 1, 0)
        y = _silu(acc[CONV_PAD:CONV_PAD + tm])
        if s < A_HEADS:
            y = y * (lax.rsqrt(jnp.sum(y * y, axis=-1, keepdims=True) + EPS) * (A_DK ** -0.5))
            q_ref[:, cols] = y
        elif s < 2 * A_HEADS:
            y = y * lax.rsqrt(jnp.sum(y * y, axis=-1, keepdims=True) + EPS)
            k_ref[:, s * LANES - A_QK_W:(s + 1) * LANES - A_QK_W] = y
        else:
            v_ref[:, s * LANES - 2 * A_QK_W:(s + 1) * LANES - 2 * A_QK_W] = y

    def gate_proj(h):
        wz = win_ref[0, :, z_col + h * A_DV:z_col + (h + 1) * A_DV]
        gate_ref[:, h * A_DV:(h + 1) * A_DV] = _silu(_mm(hb, wz))

    ab = _mm(hb, wab_ref[0])
    ab_t = ab.T

    qk_slabs = 2 * A_QK_W // PROJ_SLAB
    for p in range(qk_slabs // 2):
        project(p)

    beta_c = jax.nn.sigmoid(ab[:, 0:A_HEADS])
    g_c = -jnp.exp(alog_ref[0]) * jax.nn.softplus(ab[:, A_HEADS:2 * A_HEADS] + dtb_ref[0])
    g_r = -jnp.exp(alog_c_ref[0]) * jax.nn.softplus(ab_t[A_HEADS:2 * A_HEADS, :] + dtb_c_ref[0])

    ri = lax.broadcasted_iota(jnp.int32, (tm, tm), 0)
    ci = lax.broadcasted_iota(jnp.int32, (tm, tm), 1)
    same = (ri >> log_c) == (ci >> log_c)
    incl_c = (same & (ri >= ci)).astype(BF16)
    gcum_c = sum(jnp.dot(incl_c, part, preferred_element_type=F32) for part in _split3(g_c))
    gtot_c = jnp.concatenate([jnp.broadcast_to(gcum_c[(j + 1) * c - 1:(j + 1) * c, :], (c, A_HEADS))
                              for j in range(n_chunks)], axis=0)
    gcum_r = jnp.dot(jnp.concatenate(_split3(g_r), axis=0), (same & (ri <= ci)).astype(BF16),
                     preferred_element_type=F32)
    gcum_r = gcum_r[0:A_HEADS] + gcum_r[A_HEADS:2 * A_HEADS] + gcum_r[2 * A_HEADS:3 * A_HEADS]
    gcum_rot = [gcum_r] + [pltpu.roll(gcum_r, k * c, 1) for k in range(1, n_chunks)]

    for p in range(qk_slabs // 2, qk_slabs):
        project(p)
    for s in range(2 * A_HEADS):
        conv_slab(s)
    for p in range(qk_slabs, n_slabs):
        project(p)
    for h in range(A_HEADS // 2):
        gate_proj(h)
    for s in range(2 * A_HEADS, A_CONV_CH // LANES):
        conv_slab(s)
    for ext_ref in ext_refs:
        ext_ref[0:CONV_PAD, :] = ext_ref[tm:tm + CONV_PAD, :]

    sw = sbs * c
    row_i = lax.broadcasted_iota(jnp.int32, (c, sw), 0)
    lane_i = lax.broadcasted_iota(jnp.int32, (c, sw), 1)
    pos_i = lane_i & (c - 1)
    blk_i = lane_i >> log_c
    incl = row_i >= pos_i
    strict = row_i > pos_i
    eye = (row_i == pos_i).astype(F32)
    bd_p = ((lax.broadcasted_iota(jnp.int32, (sw, sw), 0) >> log_c)
            == (lax.broadcasted_iota(jnp.int32, (sw, sw), 1) >> log_c))
    bd_k = ((lax.broadcasted_iota(jnp.int32, (sw, sbs * A_DK), 0) >> log_c)
            == (lax.broadcasted_iota(jnp.int32, (sw, sbs * A_DK), 1) >> int(math.log2(A_DK))))

    def spread(cols):
        out = cols[:, sbs - 1:sbs]
        for hp in range(sbs - 2, -1, -1):
            out = jnp.where(blk_i == hp, cols[:, hp:hp + 1], out)
        return out

    def block_diag(m, mask):
        return jnp.where(mask, jnp.concatenate([m] * sbs, axis=0), jnp.zeros((), m.dtype))

    units = [(g, j) for g in range(A_HEADS // sbs) for j in range(n_chunks)]
    a_qk, t_inv, pw = {}, {}, {}
    for g, j in units:
        rows = slice(j * c, (j + 1) * c)
        hcols = slice(g * sbs * A_DK, (g + 1) * sbs * A_DK)
        qg = q_ref[rows, hcols].astype(BF16)
        kg = k_ref[rows, hcols].astype(BF16)
        qk_kk = lax.dot_general(jnp.concatenate([qg, kg], axis=0), block_diag(kg, bd_k), _NT,
                                preferred_element_type=F32)
        gcol = spread(gcum_c[rows, g * sbs:(g + 1) * sbs])
        grow = gcum_rot[(sbs - 1 - j) % n_chunks][g * sbs + sbs - 1:g * sbs + sbs, :]
        for hp in range(sbs - 2, -1, -1):
            grow = jnp.where(blk_i[0:1] == hp,
                             gcum_rot[(hp - j) % n_chunks][g * sbs + hp:g * sbs + hp + 1, :], grow)
        decay = jnp.where(incl, jnp.exp(jnp.minimum(gcol - grow, 0.0)), 0.0)
        a_qk[g, j] = qk_kk[0:c] * decay
        low = jnp.where(strict, qk_kk[c:2 * c] * decay * spread(beta_c[rows, g * sbs:(g + 1) * sbs]), 0.0)
        t_inv[g, j] = eye - low
        pw[g, j] = (-low).astype(BF16)

    for rnd in range(log_c):
        if 1 <= rnd <= A_HEADS // 2:
            gate_proj(A_HEADS // 2 + rnd - 1)
        for u in units:
            bd = block_diag(pw[u], bd_p)
            if rnd == 0:
                pw[u] = jnp.dot(pw[u], bd, preferred_element_type=F32).astype(BF16)
            elif rnd == log_c - 1:
                t_inv[u] = t_inv[u] + jnp.dot(t_inv[u].astype(BF16), bd, preferred_element_type=F32)
            else:
                res = jnp.dot(jnp.concatenate([pw[u], t_inv[u].astype(BF16)], axis=0), bd,
                              preferred_element_type=F32)
                pw[u] = res[0:c].astype(BF16)
                t_inv[u] = t_inv[u] + res[c:2 * c]

    uw, a_h, q_dec, k_dec = {}, {}, {}, {}
    for h in range(A_HEADS):
        g, hp = divmod(h, sbs)
        qh = q_ref[:, h * A_DK:(h + 1) * A_DK]
        kh = k_ref[:, h * A_DK:(h + 1) * A_DK]
        vh = v_ref[:, h * A_DV:(h + 1) * A_DV]
        gc = gcum_c[:, h:h + 1]
        bc = beta_c[:, h:h + 1]
        eg = jnp.exp(gc)
        rhs = jnp.concatenate([(vh * bc).astype(BF16), (kh * (bc * eg)).astype(BF16)], axis=1)
        q_dec[h] = (qh * eg).astype(BF16)
        k_dec[h] = (kh * jnp.exp(gtot_c[:, h:h + 1] - gc)).astype(BF16)
        for j in range(n_chunks):
            rows = slice(j * c, (j + 1) * c)
            t_hc = t_inv[g, j][:, hp * c:(hp + 1) * c].astype(BF16)
            uw[h, j] = jnp.dot(t_hc, rhs[rows], preferred_element_type=F32)
            a_h[h, j] = a_qk[g, j][:, hp * c:(hp + 1) * c].astype(BF16)

    st = {h: state_ref[h] for h in range(A_HEADS)}
    o_parts = {h: [] for h in range(A_HEADS)}
    for j in range(n_chunks):
        rows = slice(j * c, (j + 1) * c)
        wq = {}
        for h in range(A_HEADS):
            w = uw[h, j][:, A_DV:].astype(BF16)
            wq[h] = jnp.dot(jnp.concatenate([w, q_dec[h][rows]], axis=0), st[h].astype(BF16),
                            preferred_element_type=F32)
        for h in range(A_HEADS):
            v_new = (uw[h, j][:, 0:A_DV] - wq[h][0:c]).astype(BF16)
            o_parts[h].append(wq[h][c:2 * c] + jnp.dot(a_h[h, j], v_new, preferred_element_type=F32))
            g_last = jnp.exp(gtot_c[j * c:j * c + 1, h:h + 1])
            st[h] = st[h] * g_last + lax.dot_general(k_dec[h][rows], v_new, _TN,
                                                     preferred_element_type=F32)

    for h in range(A_HEADS):
        state_ref[h] = st[h]
        o = _rms_rows(jnp.concatenate(o_parts[h], axis=0), ogain_ref[0])
        og_ref[:, h * A_DV:(h + 1) * A_DV] = (o * gate_ref[:, h * A_DV:(h + 1) * A_DV]).astype(BF16)

    out_ref[0] = x + jnp.dot(og_ref[...], wout_ref[0], preferred_element_type=F32)


def _deltanet_layer(x, layer, norm, w_in, w_ab, conv_w, a_log, dt_bias, o_gain, w_out):
    b, t, d = x.shape
    tm = A_TILE
    n = norm.shape[0]
    in_specs = [
        pl.BlockSpec((1, tm, d), lambda bi, ti: (bi, ti, 0)),
        _layer_spec((1, d), layer),
        _layer_spec((d, A_IN_W), layer),
        _layer_spec((d, LANES), layer),
        _layer_spec((CONV_W, A_CONV_CH), layer),
        _layer_spec((1, A_HEADS), layer),
        _layer_spec((1, A_HEADS), layer),
        _layer_spec((A_HEADS, 1), layer),
        _layer_spec((A_HEADS, 1), layer),
        _layer_spec((1, A_DV), layer),
        _layer_spec((A_V_W, d), layer),
    ]
    scratch = [pltpu.VMEM((tm + CONV_PAD, PROJ_SLAB), F32)
               for _ in range(A_CONV_CH // PROJ_SLAB)] + [
        pltpu.VMEM((A_HEADS, A_DK, A_DV), F32),
        pltpu.VMEM((tm, A_QK_W), F32),
        pltpu.VMEM((tm, A_QK_W), F32),
        pltpu.VMEM((tm, A_V_W), F32),
        pltpu.VMEM((tm, A_V_W), F32),
        pltpu.VMEM((tm, A_V_W), BF16),
    ]
    return pl.pallas_call(
        _deltanet_kernel,
        out_shape=jax.ShapeDtypeStruct((b, t, d), F32),
        grid=(b, t // tm),
        in_specs=in_specs,
        out_specs=pl.BlockSpec((1, tm, d), lambda bi, ti: (bi, ti, 0)),
        scratch_shapes=scratch,
        compiler_params=pltpu.CompilerParams(
            dimension_semantics=("arbitrary", "arbitrary"), vmem_limit_bytes=VMEM_LIMIT),
        name="deltanet_layer",
    )(x, norm.reshape(n, 1, d), w_in, w_ab, conv_w,
      a_log.reshape(n, 1, A_HEADS), dt_bias.reshape(n, 1, A_HEADS),
      a_log.reshape(n, A_HEADS, 1), dt_bias.reshape(n, A_HEADS, 1),
      o_gain.reshape(n, 1, A_DV), w_out)


def _head_sum_matrix(width):
    r = lax.broadcasted_iota(jnp.int32, (width, width), 0) // B_HD
    c = lax.broadcasted_iota(jnp.int32, (width, width), 1) // B_HD
    return (r == c).astype(BF16)


def _head_rms(x, gain_tiled):
    width = x.shape[-1]
    summer = _head_sum_matrix(LANES)
    sq = (x * x).astype(BF16)
    parts = []
    for s in range(width // LANES):
        cols = slice(s * LANES, (s + 1) * LANES)
        parts.append(jnp.dot(sq[:, cols], summer, preferred_element_type=F32))
    ss = jnp.concatenate(parts, axis=-1) if len(parts) > 1 else parts[0]
    return x * lax.rsqrt(ss * (1.0 / B_HD) + EPS) * gain_tiled


def _shared_kv_kernel(x_ref, norm_ref, wkv_ref, kgain_ref, k_ref, v_ref):
    hb = _rms_rows(x_ref[0], norm_ref[...]).astype(BF16)
    kv = jnp.dot(hb, wkv_ref[...], preferred_element_type=F32)
    k_ref[0] = _head_rms(kv[:, :B_KV_W], kgain_ref[...])
    v_ref[0] = kv[:, B_KV_W:]


def _shared_kv(x, kv_norm, w_kv, k_gain):
    b, t, d = x.shape
    tk = min(KV_TILE, t)
    out = jax.ShapeDtypeStruct((b, t, B_KV_W), F32)
    spec_out = pl.BlockSpec((1, tk, B_KV_W), lambda bi, ti: (bi, ti, 0))
    return pl.pallas_call(
        _shared_kv_kernel,
        out_shape=(out, out),
        grid=(b, t // tk),
        in_specs=[pl.BlockSpec((1, tk, d), lambda bi, ti: (bi, ti, 0)),
                  _const_spec((1, d)), _const_spec((d, 2 * B_KV_W)), _const_spec((1, B_KV_W))],
        out_specs=(spec_out, spec_out),
        compiler_params=pltpu.CompilerParams(
            dimension_semantics=("arbitrary", "arbitrary"), vmem_limit_bytes=VMEM_LIMIT),
        name="shared_kv",
    )(x, kv_norm.reshape(1, d), w_kv.astype(BF16), jnp.tile(k_gain, B_KV_HEADS).reshape(1, B_KV_W))


def _bucket_ranges():
    dist = np.arange(WINDOW)
    max_exact = N_BUCKETS // 2
    large = max_exact + (np.log(np.maximum(dist, 1) / max_exact) / np.log(MAX_DIST / max_exact)
                         * (N_BUCKETS - max_exact)).astype(np.int64)
    large = np.minimum(large, N_BUCKETS - 1)
    bucket = np.where(dist < max_exact, dist, large)
    ranges = []
    for bkt in range(N_BUCKETS):
        idx = np.nonzero(bucket == bkt)[0]
        if idx.size:
            assert np.all(np.diff(idx) == 1)
            ranges.append((bkt, int(idx[0]), int(idx[-1])))
    return ranges


def _band_bias_kernel(rel_ref, out_ref):
    h0 = pl.program_id(0) * BIAS_HEADS_PER_STEP
    qi = lax.broadcasted_iota(jnp.int32, (BLOCK, BLOCK), 0)
    si = lax.broadcasted_iota(jnp.int32, (BLOCK, BLOCK), 1)
    dist = jnp.where(si > qi, qi - si + BLOCK, qi - si)
    ranges = _bucket_ranges()
    assert ranges[0][1] == 0 and ranges[-1][2] == WINDOW - 1
    in_bucket = [(bkt, (dist >= lo) & (dist <= hi)) for bkt, lo, hi in ranges]
    for i in range(BIAS_HEADS_PER_STEP):
        acc = jnp.zeros((BLOCK, BLOCK), F32)
        for bkt, hit in in_bucket:
            acc = jnp.where(hit, rel_ref[bkt, h0 + i] * LOG2E, acc)
        out_ref[i] = acc


def _band_bias(rel_bias):
    return pl.pallas_call(
        _band_bias_kernel,
        out_shape=jax.ShapeDtypeStruct((B_Q_HEADS, BLOCK, BLOCK), F32),
        grid=(B_Q_HEADS // BIAS_HEADS_PER_STEP,),
        in_specs=[pl.BlockSpec(memory_space=pltpu.SMEM)],
        out_specs=pl.BlockSpec((BIAS_HEADS_PER_STEP, BLOCK, BLOCK), lambda h: (h, 0, 0)),
        compiler_params=pltpu.CompilerParams(dimension_semantics=("arbitrary",)),
        name="band_bias",
    )(rel_bias)


def _swa_kernel(layer, sinks_ref, x_ref, norm_ref, win_ref, qgain_ref,
                kprev_ref, kcur_ref, vprev_ref, vcur_ref, bias_ref, wout_ref,
                out_ref,
                q_ref, ka_ref, kb_ref, va_ref, vb_ref, o_ref, gate_ref):
    tb = B_TILE
    first = pl.program_id(1) == 0
    x = x_ref[0]
    hb = _rms_rows(x, norm_ref[0]).astype(BF16)
    sinks = [sinks_ref[layer, hq] * LOG2E for hq in range(B_Q_HEADS)]
    group_w = SWA_HEAD_GROUP * B_HD

    def q_proj(s):
        cols = slice(s * group_w, (s + 1) * group_w)
        q = jnp.dot(hb, win_ref[0, :, cols], preferred_element_type=F32)
        q_ref[:, cols] = _head_rms(q, qgain_ref[0, :, cols] * (B_HD ** -0.5 * LOG2E)).astype(BF16)

    def gate_proj(p):
        wz = win_ref[0, :, B_W + p * GATE_SLAB:B_W + (p + 1) * GATE_SLAB]
        gate_ref[:, p * GATE_SLAB:(p + 1) * GATE_SLAB] = _silu(jnp.dot(hb, wz, preferred_element_type=F32))

    lane = lax.broadcasted_iota(jnp.int32, (BLOCK + tb, LANES), 1)
    low_half = lane < B_HD
    for t2 in range(B_KV_W // LANES):
        cols = slice(t2 * LANES, (t2 + 1) * LANES)
        for src_prev, src_cur, dst_a, dst_b in ((kprev_ref, kcur_ref, ka_ref, kb_ref),
                                                (vprev_ref, vcur_ref, va_ref, vb_ref)):
            full = jnp.concatenate([src_prev[0, :, cols], src_cur[0, :, cols]], axis=0)
            swapped = pltpu.roll(full, B_HD, 1)
            zero = jnp.zeros_like(full)
            dst_a[2 * t2] = jnp.where(low_half, full, zero).astype(BF16)
            dst_b[2 * t2] = jnp.where(low_half, zero, swapped).astype(BF16)
            dst_a[2 * t2 + 1] = jnp.where(low_half, swapped, zero).astype(BF16)
            dst_b[2 * t2 + 1] = jnp.where(low_half, zero, full).astype(BF16)

    lane_q = lax.broadcasted_iota(jnp.int32, (BLOCK, LANES), 1)
    from_prev = (lax.broadcasted_iota(jnp.int32, (BLOCK, BLOCK), 1)
                 > lax.broadcasted_iota(jnp.int32, (BLOCK, BLOCK), 0))
    pen = jnp.where(first & from_prev, NEG, 0.0)

    groups = [(qb, h0) for qb in range(tb // BLOCK) for h0 in range(0, B_Q_HEADS, SWA_HEAD_GROUP)]

    def qk_logits(qb, h0):
        rows = slice(qb * BLOCK, (qb + 1) * BLOCK)
        band = slice(qb * BLOCK, qb * BLOCK + 2 * BLOCK)
        j = h0 // B_GROUP
        kz = (ka_ref[j, band, :], kb_ref[j, band, :])
        out = {}
        for hq in range(h0, h0 + SWA_HEAD_GROUP):
            qpair = q_ref[rows, (hq // 2) * LANES:(hq // 2 + 1) * LANES]
            lg2 = lax.dot_general(qpair, kz[hq % 2], _NT, preferred_element_type=F32)
            lg = jnp.where(from_prev, lg2[:, 0:BLOCK], lg2[:, BLOCK:2 * BLOCK]) + bias_ref[hq]
            out[hq] = lg + pen if qb == 0 else lg
        return out

    def softmax_pv(qb, h0, logits):
        rows = slice(qb * BLOCK, (qb + 1) * BLOCK)
        band = slice(qb * BLOCK, qb * BLOCK + 2 * BLOCK)
        j = h0 // B_GROUP
        vz = (va_ref[j, band, :], vb_ref[j, band, :])
        heads = range(h0, h0 + SWA_HEAD_GROUP)
        m, pexp, inv = {}, {}, {}
        for hq in heads:
            m[hq] = jnp.maximum(jnp.max(logits[hq], axis=-1, keepdims=True), sinks[hq])
        for hq in heads:
            pe = jnp.exp2(logits[hq] - m[hq])
            denom = jnp.sum(pe, axis=-1, keepdims=True) + jnp.exp2(sinks[hq] - m[hq])
            inv[hq] = 1.0 / denom
            pexp[hq] = jnp.concatenate([jnp.where(from_prev, pe, 0.0).astype(BF16),
                                        jnp.where(from_prev, 0.0, pe).astype(BF16)], axis=1)
        for hq in heads[::2]:
            pv = (jnp.dot(pexp[hq], vz[0], preferred_element_type=F32)
                  + jnp.dot(pexp[hq + 1], vz[1], preferred_element_type=F32))
            p = hq // 2
            o_ref[rows, p * LANES:(p + 1) * LANES] = pv * jnp.where(lane_q < B_HD, inv[hq], inv[hq + 1])

    n_gate = B_W // GATE_SLAB
    groups_per_block = B_Q_HEADS // SWA_HEAD_GROUP
    q_proj(0)
    logits = qk_logits(*groups[0])
    for gi, grp in enumerate(groups):
        if gi + 1 < groups_per_block:
            q_proj(gi + 1)
        nxt = qk_logits(*groups[gi + 1]) if gi + 1 < len(groups) else None
        softmax_pv(*grp, logits)
        for p in range(gi * n_gate // groups_per_block, min(n_gate, (gi + 1) * n_gate // groups_per_block)):
            gate_proj(p)
        if (gi + 1) % groups_per_block == 0:
            rows = slice(grp[0] * BLOCK, (grp[0] + 1) * BLOCK)
            og = (o_ref[rows, :] * gate_ref[rows, :]).astype(BF16)
            out_ref[0, rows, :] = x_ref[0, rows, :] + jnp.dot(og, wout_ref[0], preferred_element_type=F32)
        logits = nxt


def _swa_layer(x, layer, norm, w_in, q_gain_tiled, sinks, w_out, k, v, bias):
    b, t, d = x.shape
    tb = B_TILE
    nblk = tb // BLOCK
    n = norm.shape[0]
    tile = lambda bi, ti: (bi, ti, 0)
    prev = lambda bi, ti: (bi, jnp.maximum(ti * nblk - 1, 0), 0)
    in_specs = [
        pl.BlockSpec(memory_space=pltpu.SMEM),
        pl.BlockSpec((1, tb, d), tile),
        _layer_spec((1, d), layer),
        _layer_spec((d, 2 * B_W), layer),
        _layer_spec((1, B_W), layer),
        pl.BlockSpec((1, BLOCK, B_KV_W), prev),
        pl.BlockSpec((1, tb, B_KV_W), tile),
        pl.BlockSpec((1, BLOCK, B_KV_W), prev),
        pl.BlockSpec((1, tb, B_KV_W), tile),
        _const_spec((B_Q_HEADS, BLOCK, BLOCK)),
        _layer_spec((B_W, d), layer),
    ]
    kv_scratch = pltpu.VMEM((B_KV_HEADS, BLOCK + tb, LANES), BF16)
    scratch = [pltpu.VMEM((tb, B_W), BF16), kv_scratch, kv_scratch, kv_scratch, kv_scratch,
               pltpu.VMEM((tb, B_W), F32), pltpu.VMEM((tb, B_W), F32)]
    return pl.pallas_call(
        functools.partial(_swa_kernel, layer),
        out_shape=jax.ShapeDtypeStruct((b, t, d), F32),
        grid=(b, t // tb),
        in_specs=in_specs,
        out_specs=pl.BlockSpec((1, tb, d), tile),
        scratch_shapes=scratch,
        compiler_params=pltpu.CompilerParams(
            dimension_semantics=("arbitrary", "arbitrary"), vmem_limit_bytes=VMEM_LIMIT),
        name="swa_layer",
    )(sinks, x, norm.reshape(n, 1, d), w_in, q_gain_tiled, k, k, v, v, bias, w_out)


def kernel(x, a_norm, a_w_in, a_conv, a_A_log, a_dt_bias, a_o_gain, a_w_out, kv_norm, w_kv, k_gain, rel_bias, b_norm, b_w_in, b_q_gain, b_sinks, b_w_out):
    n_a = a_w_in.shape[0]
    n_b = b_w_in.shape[0]
    a_w_in_b = a_w_in.astype(BF16)
    a_w_ab = jnp.pad(a_w_in[:, :, A_CONV_CH + A_V_W:],
                     ((0, 0), (0, 0), (0, LANES - 2 * A_HEADS))).astype(BF16)
    a_w_out_b = a_w_out.astype(BF16)
    b_w_in_b = b_w_in.astype(BF16)
    b_w_out_b = b_w_out.astype(BF16)
    q_gain_tiled = jnp.tile(b_q_gain, (1, B_Q_HEADS)).reshape(n_b, 1, B_W)
    for i in range(n_a):
        x = _deltanet_layer(x, i, a_norm, a_w_in_b, a_w_ab, a_conv, a_A_log, a_dt_bias,
                            a_o_gain, a_w_out_b)
    k, v = _shared_kv(x, kv_norm, w_kv, k_gain)
    bias = _band_bias(rel_bias)
    for j in range(n_b):
        x = _swa_layer(x, j, b_norm, b_w_in_b, q_gain_tiled, b_sinks, b_w_out_b, k, v, bias)
    return x
```

```python
import functools
import math

import numpy as np
import jax
import jax.numpy as jnp
from jax import lax
from jax.experimental import pallas as pl
from jax.experimental.pallas import tpu as pltpu

F32 = jnp.float32
BF16 = jnp.bfloat16

D_MODEL = 1024
EPS = 1e-6

A_HEADS = 8
A_DK = 128
A_DV = 256
A_QK_W = A_HEADS * A_DK
A_V_W = A_HEADS * A_DV
A_CONV_CH = 2 * A_QK_W + A_V_W
A_IN_W = A_CONV_CH + A_V_W + 2 * A_HEADS
CONV_W = 4
CHUNK = 64
A_TILE = 256
CONV_PAD = 8
PROJ_SLAB = 512
SBS_HEADS = 4

B_Q_HEADS = 32
B_KV_HEADS = 4
B_GROUP = B_Q_HEADS // B_KV_HEADS
B_HD = 64
B_W = B_Q_HEADS * B_HD
B_KV_W = B_KV_HEADS * B_HD
WINDOW = 128
BLOCK = 128
B_TILE = 512
SWA_HEAD_GROUP = 8
KV_TILE = 1024
N_BUCKETS = 32
MAX_DIST = 128
NEG = -1e30
LOG2E = math.log2(math.e)
GATE_SLAB = 256
BIAS_HEADS_PER_STEP = 8

LANES = 128
VMEM_LIMIT = 56 * 1024 * 1024

_NT = (((1,), (1,)), ((), ()))
_NN = (((1,), (0,)), ((), ()))
_TN = (((0,), (0,)), ((), ()))


def _mm(a, b, dims=_NN):
    return lax.dot_general(a.astype(BF16), b.astype(BF16), dims, preferred_element_type=F32)


def _split3(a):
    hi = a.astype(BF16)
    rest = a - hi.astype(F32)
    mid = rest.astype(BF16)
    lo = (rest - mid.astype(F32)).astype(BF16)
    return hi, mid, lo


def _silu(x):
    h = 0.5 * x
    return h + h * jnp.tanh(h)


def _rms_rows(x, gain):
    return x * lax.rsqrt(jnp.mean(x * x, axis=-1, keepdims=True) + EPS) * gain


def _const_spec(shape):
    nd = len(shape)
    return pl.BlockSpec(shape, lambda *_: (0,) * nd, pipeline_mode=pl.Buffered(1))


def _layer_spec(shape, layer):
    nd = len(shape)
    return pl.BlockSpec((1,) + tuple(shape), lambda *_: (layer,) + (0,) * nd,
                        pipeline_mode=pl.Buffered(1))


def _deltanet_kernel(x_ref, norm_ref, win_ref, wab_ref, conv_ref,
                     alog_ref, dtb_ref, alog_c_ref, dtb_c_ref, ogain_ref, wout_ref,
                     out_ref, *scratch):
    n_slabs = A_CONV_CH // PROJ_SLAB
    ext_refs = scratch[:n_slabs]
    state_ref, q_ref, k_ref, v_ref, gate_ref, og_ref = scratch[n_slabs:]
    tm = A_TILE
    c = CHUNK
    sbs = SBS_HEADS
    n_chunks = tm // c
    log_c = int(math.log2(c))
    assert n_chunks == sbs
    first = pl.program_id(1) == 0

    @pl.when(first)
    def _():
        for ext_ref in ext_refs:
            ext_ref[0:CONV_PAD, :] = jnp.zeros((CONV_PAD, PROJ_SLAB), F32)
        state_ref[...] = jnp.zeros_like(state_ref)

    x = x_ref[0]
    hb = _rms_rows(x, norm_ref[0]).astype(BF16)
    z_col = A_CONV_CH


    def project(p):
        ext_refs[p][CONV_PAD:CONV_PAD + tm, :] = _mm(hb, win_ref[0, :, p * PROJ_SLAB:(p + 1) * PROJ_SLAB])

    def conv_slab(s):
        cols = slice(s * LANES, (s + 1) * LANES)
        p, off = divmod(s * LANES, PROJ_SLAB)
        ext = ext_refs[p][:, off:off + LANES]
        acc = ext * conv_ref[0, 0:1, cols]
        for j in range(1, CONV_W):
            acc = ext * conv_ref[0, j:j + 1, cols] + pltpu.roll(acc, 1, 0)
        y = _silu(acc[CONV_PAD:CONV_PAD + tm])
        if s < A_HEADS:
            y = y * (lax.rsqrt(jnp.sum(y * y, axis=-1, keepdims=True) + EPS) * (A_DK ** -0.5))
            q_ref[:, cols] = y
        elif s < 2 * A_HEADS:
            y = y * lax.rsqrt(jnp.sum(y * y, axis=-1, keepdims=True) + EPS)
            k_ref[:, s * LANES - A_QK_W:(s + 1) * LANES - A_QK_W] = y
        else:
            v_ref[:, s * LANES - 2 * A_QK_W:(s + 1) * LANES - 2 * A_QK_W] = y

    def gate_proj(h):
        wz = win_ref[0, :, z_col + h * A_DV:z_col + (h + 1) * A_DV]
        gate_ref[:, h * A_DV:(h + 1) * A_DV] = _silu(_mm(hb, wz))

    ab = _mm(hb, wab_ref[0])
    ab_t = ab.T

    qk_slabs = 2 * A_QK_W // PROJ_SLAB
    for p in range(qk_slabs // 2):
        project(p)

    beta_c = jax.nn.sigmoid(ab[:, 0:A_HEADS])
    g_c = -jnp.exp(alog_ref[0]) * jax.nn.softplus(ab[:, A_HEADS:2 * A_HEADS] + dtb_ref[0])
    g_r = -jnp.exp(alog_c_ref[0]) * jax.nn.softplus(ab_t[A_HEADS:2 * A_HEADS, :] + dtb_c_ref[0])

    ri = lax.broadcasted_iota(jnp.int32, (tm, tm), 0)
    ci = lax.broadcasted_iota(jnp.int32, (tm, tm), 1)
    same = (ri >> log_c) == (ci >> log_c)
    incl_c = (same & (ri >= ci)).astype(BF16)
    gcum_c = sum(jnp.dot(incl_c, part, preferred_element_type=F32) for part in _split3(g_c))
    gtot_c = jnp.concatenate([jnp.broadcast_to(gcum_c[(j + 1) * c - 1:(j + 1) * c, :], (c, A_HEADS))
                              for j in range(n_chunks)], axis=0)
    gcum_r = jnp.dot(jnp.concatenate(_split3(g_r), axis=0), (same & (ri <= ci)).astype(BF16),
                     preferred_element_type=F32)
    gcum_r = gcum_r[0:A_HEADS] + gcum_r[A_HEADS:2 * A_HEADS] + gcum_r[2 * A_HEADS:3 * A_HEADS]
    gcum_rot = [gcum_r] + [pltpu.roll(gcum_r, k * c, 1) for k in range(1, n_chunks)]

    for p in range(qk_slabs // 2, qk_slabs):
        project(p)
    for s in range(2 * A_HEADS):
        conv_slab(s)
    for p in range(qk_slabs, n_slabs):
        project(p)
    for h in range(A_HEADS // 2):
        gate_proj(h)
    for s in range(2 * A_HEADS, A_CONV_CH // LANES):
        conv_slab(s)
    for ext_ref in ext_refs:
        ext_ref[0:CONV_PAD, :] = ext_ref[tm:tm + CONV_PAD, :]

    sw = sbs * c
    row_i = lax.broadcasted_iota(jnp.int32, (c, sw), 0)
    lane_i = lax.broadcasted_iota(jnp.int32, (c, sw), 1)
    pos_i = lane_i & (c - 1)
    blk_i = lane_i >> log_c
    incl = row_i >= pos_i
    strict = row_i > pos_i
    eye = (row_i == pos_i).astype(F32)
    bd_p = ((lax.broadcasted_iota(jnp.int32, (sw, sw), 0) >> log_c)
            == (lax.broadcasted_iota(jnp.int32, (sw, sw), 1) >> log_c))
    bd_k = ((lax.broadcasted_iota(jnp.int32, (sw, sbs * A_DK), 0) >> log_c)
            == (lax.broadcasted_iota(jnp.int32, (sw, sbs * A_DK), 1) >> int(math.log2(A_DK))))

    def spread(cols):
        out = cols[:, sbs - 1:sbs]
        for hp in range(sbs - 2, -1, -1):
            out = jnp.where(blk_i == hp, cols[:, hp:hp + 1], out)
        return out

    def block_diag(m, mask):
        return jnp.where(mask, jnp.concatenate([m] * sbs, axis=0), jnp.zeros((), m.dtype))

    units = [(g, j) for g in range(A_HEADS // sbs) for j in range(n_chunks)]
    a_qk, t_inv, pw = {}, {}, {}
    for g, j in units:
        rows = slice(j * c, (j + 1) * c)
        hcols = slice(g * sbs * A_DK, (g + 1) * sbs * A_DK)
        qg = q_ref[rows, hcols].astype(BF16)
        kg = k_ref[rows, hcols].astype(BF16)
        qk_kk = lax.dot_general(jnp.concatenate([qg, kg], axis=0), block_diag(kg, bd_k), _NT,
                                preferred_element_type=F32)
        gcol = spread(gcum_c[rows, g * sbs:(g + 1) * sbs])
        grow = gcum_rot[(sbs - 1 - j) % n_chunks][g * sbs + sbs - 1:g * sbs + sbs, :]
        for hp in range(sbs - 2, -1, -1):
            grow = jnp.where(blk_i[0:1] == hp,
                             gcum_rot[(hp - j) % n_chunks][g * sbs + hp:g * sbs + hp + 1, :], grow)
        decay = jnp.where(incl, jnp.exp(jnp.minimum(gcol - grow, 0.0)), 0.0)
        a_qk[g, j] = qk_kk[0:c] * decay
        low = jnp.where(strict, qk_kk[c:2 * c] * decay * spread(beta_c[rows, g * sbs:(g + 1) * sbs]), 0.0)
        t_inv[g, j] = eye - low
        pw[g, j] = (-low).astype(BF16)

    for rnd in range(log_c):
        if 1 <= rnd <= A_HEADS // 2:
            gate_proj(A_HEADS // 2 + rnd - 1)
        for u in units:
            bd = block_diag(pw[u], bd_p)
            if rnd == 0:
                pw[u] = jnp.dot(pw[u], bd, preferred_element_type=F32).astype(BF16)
            elif rnd == log_c - 1:
                t_inv[u] = t_inv[u] + jnp.dot(t_inv[u].astype(BF16), bd, preferred_element_type=F32)
            else:
                res = jnp.dot(jnp.concatenate([pw[u], t_inv[u].astype(BF16)], axis=0), bd,
                              preferred_element_type=F32)
                pw[u] = res[0:c].astype(BF16)
                t_inv[u] = t_inv[u] + res[c:2 * c]

    st = {h: state_ref[h] for h in range(A_HEADS)}
    o_parts = {h: [] for h in range(A_HEADS)}
    for j in range(n_chunks):
        rows = slice(j * c, (j + 1) * c)
        uw, a_h, q_dec, k_dec = {}, {}, {}, {}
        for h in range(A_HEADS):
            g, hp = divmod(h, sbs)
            qh = q_ref[rows, h * A_DK:(h + 1) * A_DK]
            kh = k_ref[rows, h * A_DK:(h + 1) * A_DK]
            vh = v_ref[rows, h * A_DV:(h + 1) * A_DV]
            gc = gcum_c[rows, h:h + 1]
            bc = beta_c[rows, h:h + 1]
            eg = jnp.exp(gc)
            rhs = jnp.concatenate([(vh * bc).astype(BF16), (kh * (bc * eg)).astype(BF16)], axis=1)
            q_dec[h] = (qh * eg).astype(BF16)
            k_dec[h] = (kh * jnp.exp(gtot_c[rows, h:h + 1] - gc)).astype(BF16)
            t_hc = t_inv[g, j][:, hp * c:(hp + 1) * c].astype(BF16)
            uw[h] = jnp.dot(t_hc, rhs, preferred_element_type=F32)
            a_h[h] = a_qk[g, j][:, hp * c:(hp + 1) * c].astype(BF16)
        wq = {}
        for h in range(A_HEADS):
            w = uw[h][:, A_DV:].astype(BF16)
            wq[h] = jnp.dot(jnp.concatenate([w, q_dec[h]], axis=0), st[h].astype(BF16),
                            preferred_element_type=F32)
        for h in range(A_HEADS):
            v_new = (uw[h][:, 0:A_DV] - wq[h][0:c]).astype(BF16)
            o_parts[h].append(wq[h][c:2 * c] + jnp.dot(a_h[h], v_new, preferred_element_type=F32))
            g_last = jnp.exp(gtot_c[j * c:j * c + 1, h:h + 1])
            st[h] = st[h] * g_last + lax.dot_general(k_dec[h], v_new, _TN,
                                                     preferred_element_type=F32)

    for h in range(A_HEADS):
        state_ref[h] = st[h]
        o = _rms_rows(jnp.concatenate(o_parts[h], axis=0), ogain_ref[0])
        og_ref[:, h * A_DV:(h + 1) * A_DV] = (o * gate_ref[:, h * A_DV:(h + 1) * A_DV]).astype(BF16)

    out_ref[0] = x + jnp.dot(og_ref[...], wout_ref[0], preferred_element_type=F32)


def _deltanet_layer(x, layer, norm, w_in, w_ab, conv_w, a_log, dt_bias, o_gain, w_out):
    b, t, d = x.shape
    tm = A_TILE
    n = norm.shape[0]
    in_specs = [
        pl.BlockSpec((1, tm, d), lambda bi, ti: (bi, ti, 0)),
        _layer_spec((1, d), layer),
        _layer_spec((d, A_IN_W), layer),
        _layer_spec((d, LANES), layer),
        _layer_spec((CONV_W, A_CONV_CH), layer),
        _layer_spec((1, A_HEADS), layer),
        _layer_spec((1, A_HEADS), layer),
        _layer_spec((A_HEADS, 1), layer),
        _layer_spec((A_HEADS, 1), layer),
        _layer_spec((1, A_DV), layer),
        _layer_spec((A_V_W, d), layer),
    ]
    scratch = [pltpu.VMEM((tm + CONV_PAD, PROJ_SLAB), F32)
               for _ in range(A_CONV_CH // PROJ_SLAB)] + [
        pltpu.VMEM((A_HEADS, A_DK, A_DV), F32),
        pltpu.VMEM((tm, A_QK_W), F32),
        pltpu.VMEM((tm, A_QK_W), F32),
        pltpu.VMEM((tm, A_V_W), F32),
        pltpu.VMEM((tm, A_V_W), F32),
        pltpu.VMEM((tm, A_V_W), BF16),
    ]
    return pl.pallas_call(
        _deltanet_kernel,
        out_shape=jax.ShapeDtypeStruct((b, t, d), F32),
        grid=(b, t // tm),
        in_specs=in_specs,
        out_specs=pl.BlockSpec((1, tm, d), lambda bi, ti: (bi, ti, 0)),
        scratch_shapes=scratch,
        compiler_params=pltpu.CompilerParams(
            dimension_semantics=("arbitrary", "arbitrary"), vmem_limit_bytes=VMEM_LIMIT),
        name="deltanet_layer",
    )(x, norm.reshape(n, 1, d), w_in, w_ab, conv_w,
      a_log.reshape(n, 1, A_HEADS), dt_bias.reshape(n, 1, A_HEADS),
      a_log.reshape(n, A_HEADS, 1), dt_bias.reshape(n, A_HEADS, 1),
      o_gain.reshape(n, 1, A_DV), w_out)


def _head_sum_matrix(width):
    r = lax.broadcasted_iota(jnp.int32, (width, width), 0) // B_HD
    c = lax.broadcasted_iota(jnp.int32, (width, width), 1) // B_HD
    return (r == c).astype(BF16)


def _head_rms(x, gain_tiled):
    width = x.shape[-1]
    summer = _head_sum_matrix(LANES)
    sq = (x * x).astype(BF16)
    parts = []
    for s in range(width // LANES):
        cols = slice(s * LANES, (s + 1) * LANES)
        parts.append(jnp.dot(sq[:, cols], summer, preferred_element_type=F32))
    ss = jnp.concatenate(parts, axis=-1) if len(parts) > 1 else parts[0]
    return x * lax.rsqrt(ss * (1.0 / B_HD) + EPS) * gain_tiled


def _shared_kv_kernel(x_ref, norm_ref, wkv_ref, kgain_ref, k_ref, v_ref):
    hb = _rms_rows(x_ref[0], norm_ref[...]).astype(BF16)
    kv = jnp.dot(hb, wkv_ref[...], preferred_element_type=F32)
    k_ref[0] = _head_rms(kv[:, :B_KV_W], kgain_ref[...])
    v_ref[0] = kv[:, B_KV_W:]


def _shared_kv(x, kv_norm, w_kv, k_gain):
    b, t, d = x.shape
    tk = min(KV_TILE, t)
    out = jax.ShapeDtypeStruct((b, t, B_KV_W), F32)
    spec_out = pl.BlockSpec((1, tk, B_KV_W), lambda bi, ti: (bi, ti, 0))
    return pl.pallas_call(
        _shared_kv_kernel,
        out_shape=(out, out),
        grid=(b, t // tk),
        in_specs=[pl.BlockSpec((1, tk, d), lambda bi, ti: (bi, ti, 0)),
                  _const_spec((1, d)), _const_spec((d, 2 * B_KV_W)), _const_spec((1, B_KV_W))],
        out_specs=(spec_out, spec_out),
        compiler_params=pltpu.CompilerParams(
            dimension_semantics=("arbitrary", "arbitrary"), vmem_limit_bytes=VMEM_LIMIT),
        name="shared_kv",
    )(x, kv_norm.reshape(1, d), w_kv.astype(BF16), jnp.tile(k_gain, B_KV_HEADS).reshape(1, B_KV_W))


def _bucket_ranges():
    dist = np.arange(WINDOW)
    max_exact = N_BUCKETS // 2
    large = max_exact + (np.log(np.maximum(dist, 1) / max_exact) / np.log(MAX_DIST / max_exact)
                         * (N_BUCKETS - max_exact)).astype(np.int64)
    large = np.minimum(large, N_BUCKETS - 1)
    bucket = np.where(dist < max_exact, dist, large)
    ranges = []
    for bkt in range(N_BUCKETS):
        idx = np.nonzero(bucket == bkt)[0]
        if idx.size:
            assert np.all(np.diff(idx) == 1)
            ranges.append((bkt, int(idx[0]), int(idx[-1])))
    return ranges


def _band_bias_kernel(rel_ref, out_ref):
    h0 = pl.program_id(0) * BIAS_HEADS_PER_STEP
    qi = lax.broadcasted_iota(jnp.int32, (BLOCK, BLOCK), 0)
    si = lax.broadcasted_iota(jnp.int32, (BLOCK, BLOCK), 1)
    dist = jnp.where(si > qi, qi - si + BLOCK, qi - si)
    ranges = _bucket_ranges()
    assert ranges[0][1] == 0 and ranges[-1][2] == WINDOW - 1
    in_bucket = [(bkt, (dist >= lo) & (dist <= hi)) for bkt, lo, hi in ranges]
    for i in range(BIAS_HEADS_PER_STEP):
        acc = jnp.zeros((BLOCK, BLOCK), F32)
        for bkt, hit in in_bucket:
            acc = jnp.where(hit, rel_ref[bkt, h0 + i] * LOG2E, acc)
        out_ref[i] = acc


def _band_bias(rel_bias):
    return pl.pallas_call(
        _band_bias_kernel,
        out_shape=jax.ShapeDtypeStruct((B_Q_HEADS, BLOCK, BLOCK), F32),
        grid=(B_Q_HEADS // BIAS_HEADS_PER_STEP,),
        in_specs=[pl.BlockSpec(memory_space=pltpu.SMEM)],
        out_specs=pl.BlockSpec((BIAS_HEADS_PER_STEP, BLOCK, BLOCK), lambda h: (h, 0, 0)),
        compiler_params=pltpu.CompilerParams(dimension_semantics=("arbitrary",)),
        name="band_bias",
    )(rel_bias)


def _swa_kernel(layer, sinks_ref, x_ref, norm_ref, win_ref, qgain_ref,
                kprev_ref, kcur_ref, vprev_ref, vcur_ref, bias_ref, wout_ref,
                out_ref,
                q_ref, ka_ref, kb_ref, va_ref, vb_ref, o_ref, gate_ref):
    tb = B_TILE
    first = pl.program_id(1) == 0
    x = x_ref[0]
    hb = _rms_rows(x, norm_ref[0]).astype(BF16)
    sinks = [sinks_ref[layer, hq] * LOG2E for hq in range(B_Q_HEADS)]
    group_w = SWA_HEAD_GROUP * B_HD

    def q_proj(s):
        cols = slice(s * group_w, (s + 1) * group_w)
        q = jnp.dot(hb, win_ref[0, :, cols], preferred_element_type=F32)
        q_ref[:, cols] = _head_rms(q, qgain_ref[0, :, cols] * (B_HD ** -0.5 * LOG2E)).astype(BF16)

    def gate_proj(p):
        wz = win_ref[0, :, B_W + p * GATE_SLAB:B_W + (p + 1) * GATE_SLAB]
        gate_ref[:, p * GATE_SLAB:(p + 1) * GATE_SLAB] = _silu(jnp.dot(hb, wz, preferred_element_type=F32))

    lane = lax.broadcasted_iota(jnp.int32, (BLOCK + tb, LANES), 1)
    low_half = lane < B_HD
    for t2 in range(B_KV_W // LANES):
        cols = slice(t2 * LANES, (t2 + 1) * LANES)
        for src_prev, src_cur, dst_a, dst_b in ((kprev_ref, kcur_ref, ka_ref, kb_ref),
                                                (vprev_ref, vcur_ref, va_ref, vb_ref)):
            full = jnp.concatenate([src_prev[0, :, cols], src_cur[0, :, cols]], axis=0)
            swapped = pltpu.roll(full, B_HD, 1)
            zero = jnp.zeros_like(full)
            dst_a[2 * t2] = jnp.where(low_half, full, zero).astype(BF16)
            dst_b[2 * t2] = jnp.where(low_half, zero, swapped).astype(BF16)
            dst_a[2 * t2 + 1] = jnp.where(low_half, swapped, zero).astype(BF16)
            dst_b[2 * t2 + 1] = jnp.where(low_half, zero, full).astype(BF16)

    lane_q = lax.broadcasted_iota(jnp.int32, (BLOCK, LANES), 1)
    from_prev = (lax.broadcasted_iota(jnp.int32, (BLOCK, BLOCK), 1)
                 > lax.broadcasted_iota(jnp.int32, (BLOCK, BLOCK), 0))
    pen = jnp.where(first & from_prev, NEG, 0.0)

    groups = [(qb, h0) for qb in range(tb // BLOCK) for h0 in range(0, B_Q_HEADS, SWA_HEAD_GROUP)]

    def qk_logits(qb, h0):
        rows = slice(qb * BLOCK, (qb + 1) * BLOCK)
        band = slice(qb * BLOCK, qb * BLOCK + 2 * BLOCK)
        j = h0 // B_GROUP
        kz = (ka_ref[j, band, :], kb_ref[j, band, :])
        out = {}
        for hq in range(h0, h0 + SWA_HEAD_GROUP):
            qpair = q_ref[rows, (hq // 2) * LANES:(hq // 2 + 1) * LANES]
            lg2 = lax.dot_general(qpair, kz[hq % 2], _NT, preferred_element_type=F32)
            lg = jnp.where(from_prev, lg2[:, 0:BLOCK], lg2[:, BLOCK:2 * BLOCK]) + bias_ref[hq]
            out[hq] = lg + pen if qb == 0 else lg
        return out

    def softmax_pv(qb, h0, logits):
        rows = slice(qb * BLOCK, (qb + 1) * BLOCK)
        band = slice(qb * BLOCK, qb * BLOCK + 2 * BLOCK)
        j = h0 // B_GROUP
        vz = (va_ref[j, band, :], vb_ref[j, band, :])
        heads = range(h0, h0 + SWA_HEAD_GROUP)
        m, pexp, inv = {}, {}, {}
        for hq in heads:
            m[hq] = jnp.maximum(jnp.max(logits[hq], axis=-1, keepdims=True), sinks[hq])
        for hq in heads:
            pe = jnp.exp2(logits[hq] - m[hq])
            denom = jnp.sum(pe, axis=-1, keepdims=True) + jnp.exp2(sinks[hq] - m[hq])
            inv[hq] = 1.0 / denom
            pexp[hq] = jnp.concatenate([jnp.where(from_prev, pe, 0.0).astype(BF16),
                                        jnp.where(from_prev, 0.0, pe).astype(BF16)], axis=1)
        for hq in heads[::2]:
            pv = (jnp.dot(pexp[hq], vz[0], preferred_element_type=F32)
                  + jnp.dot(pexp[hq + 1], vz[1], preferred_element_type=F32))
            p = hq // 2
            o_ref[rows, p * LANES:(p + 1) * LANES] = pv * jnp.where(lane_q < B_HD, inv[hq], inv[hq + 1])

    n_gate = B_W // GATE_SLAB
    groups_per_block = B_Q_HEADS // SWA_HEAD_GROUP
    q_proj(0)
    logits = qk_logits(*groups[0])
    for gi, grp in enumerate(groups):
        if gi + 1 < groups_per_block:
            q_proj(gi + 1)
        nxt = qk_logits(*groups[gi + 1]) if gi + 1 < len(groups) else None
        softmax_pv(*grp, logits)
        for p in range(gi * n_gate // groups_per_block, min(n_gate, (gi + 1) * n_gate // groups_per_block)):
            gate_proj(p)
        if (gi + 1) % groups_per_block == 0:
            rows = slice(grp[0] * BLOCK, (grp[0] + 1) * BLOCK)
            og = (o_ref[rows, :] * gate_ref[rows, :]).astype(BF16)
            out_ref[0, rows, :] = x_ref[0, rows, :] + jnp.dot(og, wout_ref[0], preferred_element_type=F32)
        logits = nxt


def _swa_layer(x, layer, norm, w_in, q_gain_tiled, sinks, w_out, k, v, bias):
    b, t, d = x.shape
    tb = B_TILE
    nblk = tb // BLOCK
    n = norm.shape[0]
    tile = lambda bi, ti: (bi, ti, 0)
    prev = lambda bi, ti: (bi, jnp.maximum(ti * nblk - 1, 0), 0)
    in_specs = [
        pl.BlockSpec(memory_space=pltpu.SMEM),
        pl.BlockSpec((1, tb, d), tile),
        _layer_spec((1, d), layer),
        _layer_spec((d, 2 * B_W), layer),
        _layer_spec((1, B_W), layer),
        pl.BlockSpec((1, BLOCK, B_KV_W), prev),
        pl.BlockSpec((1, tb, B_KV_W), tile),
        pl.BlockSpec((1, BLOCK, B_KV_W), prev),
        pl.BlockSpec((1, tb, B_KV_W), tile),
        _const_spec((B_Q_HEADS, BLOCK, BLOCK)),
        _layer_spec((B_W, d), layer),
    ]
    kv_scratch = pltpu.VMEM((B_KV_HEADS, BLOCK + tb, LANES), BF16)
    scratch = [pltpu.VMEM((tb, B_W), BF16), kv_scratch, kv_scratch, kv_scratch, kv_scratch,
               pltpu.VMEM((tb, B_W), F32), pltpu.VMEM((tb, B_W), F32)]
    return pl.pallas_call(
        functools.partial(_swa_kernel, layer),
        out_shape=jax.ShapeDtypeStruct((b, t, d), F32),
        grid=(b, t // tb),
        in_specs=in_specs,
        out_specs=pl.BlockSpec((1, tb, d), tile),
        scratch_shapes=scratch,
        compiler_params=pltpu.CompilerParams(
            dimension_semantics=("arbitrary", "arbitrary"), vmem_limit_bytes=VMEM_LIMIT),
        name="swa_layer",
    )(sinks, x, norm.reshape(n, 1, d), w_in, q_gain_tiled, k, k, v, v, bias, w_out)


def kernel(x, a_norm, a_w_in, a_conv, a_A_log, a_dt_bias, a_o_gain, a_w_out, kv_norm, w_kv, k_gain, rel_bias, b_norm, b_w_in, b_q_gain, b_sinks, b_w_out):
    n_a = a_w_in.shape[0]
    n_b = b_w_in.shape[0]
    a_w_in_b = a_w_in.astype(BF16)
    a_w_ab = jnp.pad(a_w_in[:, :, A_CONV_CH + A_V_W:],
                     ((0, 0), (0, 0), (0, LANES - 2 * A_HEADS))).astype(BF16)
    a_w_out_b = a_w_out.astype(BF16)
    b_w_in_b = b_w_in.astype(BF16)
    b_w_out_b = b_w_out.astype(BF16)
    q_gain_tiled = jnp.tile(b_q_gain, (1, B_Q_HEADS)).reshape(n_b, 1, B_W)
    for i in range(n_a):
        x = _deltanet_layer(x, i, a_norm, a_w_in_b, a_w_ab, a_conv, a_A_log, a_dt_bias,
                            a_o_gain, a_w_out_b)
    k, v = _shared_kv(x, kv_norm, w_kv, k_gain)
    bias = _band_bias(rel_bias)
    for j in range(n_b):
        x = _swa_layer(x, j, b_norm, b_w_in_b, q_gain_tiled, b_sinks, b_w_out_b, k, v, bias)
    return x
```

```python
import functools
import math

import numpy as np
import jax
import jax.numpy as jnp
from jax import lax
from jax.experimental import pallas as pl
from jax.experimental.pallas import tpu as pltpu

F32 = jnp.float32
BF16 = jnp.bfloat16

D_MODEL = 1024
EPS = 1e-6

A_HEADS = 8
A_DK = 128
A_DV = 256
A_QK_W = A_HEADS * A_DK
A_V_W = A_HEADS * A_DV
A_CONV_CH = 2 * A_QK_W + A_V_W
A_IN_W = A_CONV_CH + A_V_W + 2 * A_HEADS
CONV_W = 4
CHUNK = 64
A_TILE = 256
CONV_PAD = 8
PROJ_SLAB = 512
SBS_HEADS = 4

B_Q_HEADS = 32
B_KV_HEADS = 4
B_GROUP = B_Q_HEADS // B_KV_HEADS
B_HD = 64
B_W = B_Q_HEADS * B_HD
B_KV_W = B_KV_HEADS * B_HD
WINDOW = 128
BLOCK = 128
B_TILE = 512
SWA_HEAD_GROUP = 8
KV_TILE = 1024
N_BUCKETS = 32
MAX_DIST = 128
NEG = -1e30
LOG2E = math.log2(math.e)
GATE_SLAB = 256
BIAS_HEADS_PER_STEP = 8

LANES = 128
VMEM_LIMIT = 56 * 1024 * 1024

_NT = (((1,), (1,)), ((), ()))
_NN = (((1,), (0,)), ((), ()))
_TN = (((0,), (0,)), ((), ()))


def _mm(a, b, dims=_NN):
    return lax.dot_general(a.astype(BF16), b.astype(BF16), dims, preferred_element_type=F32)


def _split3(a):
    hi = a.astype(BF16)
    rest = a - hi.astype(F32)
    mid = rest.astype(BF16)
    lo = (rest - mid.astype(F32)).astype(BF16)
    return hi, mid, lo


def _silu(x):
    h = 0.5 * x
    return h + h * jnp.tanh(h)


def _rms_rows(x, gain):
    return x * lax.rsqrt(jnp.mean(x * x, axis=-1, keepdims=True) + EPS) * gain


def _const_spec(shape):
    nd = len(shape)
    return pl.BlockSpec(shape, lambda *_: (0,) * nd, pipeline_mode=pl.Buffered(1))


def _layer_spec(shape, layer):
    nd = len(shape)
    return pl.BlockSpec((1,) + tuple(shape), lambda *_: (layer,) + (0,) * nd,
                        pipeline_mode=pl.Buffered(1))


def _deltanet_kernel(x_ref, norm_ref, win_ref, wab_ref, conv_ref,
                     alog_ref, dtb_ref, alog_c_ref, dtb_c_ref, ogain_ref, wout_ref,
                     out_ref, *scratch):
    n_slabs = A_CONV_CH // PROJ_SLAB
    ext_refs = scratch[:n_slabs]
    state_ref, q_ref, k_ref, v_ref, gate_ref, og_ref = scratch[n_slabs:]
    tm = A_TILE
    c = CHUNK
    sbs = SBS_HEADS
    n_chunks = tm // c
    log_c = int(math.log2(c))
    assert n_chunks == sbs
    first = pl.program_id(1) == 0

    @pl.when(first)
    def _():
        for ext_ref in ext_refs:
            ext_ref[0:CONV_PAD, :] = jnp.zeros((CONV_PAD, PROJ_SLAB), F32)
        state_ref[...] = jnp.zeros_like(state_ref)

    x = x_ref[0]
    hb = _rms_rows(x, norm_ref[0]).astype(BF16)
    z_col = A_CONV_CH


    def project(p):
        ext_refs[p][CONV_PAD:CONV_PAD + tm, :] = _mm(hb, win_ref[0, :, p * PROJ_SLAB:(p + 1) * PROJ_SLAB])

    def conv_slab(s):
        cols = slice(s * LANES, (s + 1) * LANES)
        p, off = divmod(s * LANES, PROJ_SLAB)
        ext = ext_refs[p][:, off:off + LANES]
        acc = ext * conv_ref[0, 0:1, cols]
        for j in range(1, CONV_W):
            acc = ext * conv_ref[0, j:j + 1, cols] + pltpu.roll(acc, 1, 0)
        y = _silu(acc[CONV_PAD:CONV_PAD + tm])
        if s < A_HEADS:
            y = y * (lax.rsqrt(jnp.sum(y * y, axis=-1, keepdims=True) + EPS) * (A_DK ** -0.5))
            q_ref[:, cols] = y
        elif s < 2 * A_HEADS:
            y = y * lax.rsqrt(jnp.sum(y * y, axis=-1, keepdims=True) + EPS)
            k_ref[:, s * LANES - A_QK_W:(s + 1) * LANES - A_QK_W] = y
        else:
            v_ref[:, s * LANES - 2 * A_QK_W:(s + 1) * LANES - 2 * A_QK_W] = y

    def gate_proj(h):
        wz = win_ref[0, :, z_col + h * A_DV:z_col + (h + 1) * A_DV]
        gate_ref[:, h * A_DV:(h + 1) * A_DV] = _silu(_mm(hb, wz))

    ab = _mm(hb, wab_ref[0])
    ab_t = ab.T

    qk_slabs = 2 * A_QK_W // PROJ_SLAB
    for p in range(qk_slabs // 2):
        project(p)

    beta_c = jax.nn.sigmoid(ab[:, 0:A_HEADS])
    g_c = -jnp.exp(alog_ref[0]) * jax.nn.softplus(ab[:, A_HEADS:2 * A_HEADS] + dtb_ref[0])
    g_r = -jnp.exp(alog_c_ref[0]) * jax.nn.softplus(ab_t[A_HEADS:2 * A_HEADS, :] + dtb_c_ref[0])

    ri = lax.broadcasted_iota(jnp.int32, (tm, tm), 0)
    ci = lax.broadcasted_iota(jnp.int32, (tm, tm), 1)
    same = (ri >> log_c) == (ci >> log_c)
    incl_c = (same & (ri >= ci)).astype(BF16)
    gcum_c = sum(jnp.dot(incl_c, part, preferred_element_type=F32) for part in _split3(g_c))
    gtot_c = jnp.concatenate([jnp.broadcast_to(gcum_c[(j + 1) * c - 1:(j + 1) * c, :], (c, A_HEADS))
                              for j in range(n_chunks)], axis=0)
    gcum_r = jnp.dot(jnp.concatenate(_split3(g_r), axis=0), (same & (ri <= ci)).astype(BF16),
                     preferred_element_type=F32)
    gcum_r = gcum_r[0:A_HEADS] + gcum_r[A_HEADS:2 * A_HEADS] + gcum_r[2 * A_HEADS:3 * A_HEADS]
    gcum_rot = [gcum_r] + [pltpu.roll(gcum_r, k * c, 1) for k in range(1, n_chunks)]

    for p in range(qk_slabs // 2, qk_slabs):
        project(p)
    for s in range(2 * A_HEADS):
        conv_slab(s)
    for p in range(qk_slabs, n_slabs):
        project(p)
    for h in range(A_HEADS // 2):
        gate_proj(h)
    for s in range(2 * A_HEADS, A_CONV_CH // LANES):
        conv_slab(s)
    for ext_ref in ext_refs:
        ext_ref[0:CONV_PAD, :] = ext_ref[tm:tm + CONV_PAD, :]

    sw = sbs * c
    row_i = lax.broadcasted_iota(jnp.int32, (c, sw), 0)
    lane_i = lax.broadcasted_iota(jnp.int32, (c, sw), 1)
    pos_i = lane_i & (c - 1)
    blk_i = lane_i >> log_c
    incl = row_i >= pos_i
    strict = row_i > pos_i
    eye = (row_i == pos_i).astype(F32)
    bd_p = ((lax.broadcasted_iota(jnp.int32, (sw, sw), 0) >> log_c)
            == (lax.broadcasted_iota(jnp.int32, (sw, sw), 1) >> log_c))
    bd_k = ((lax.broadcasted_iota(jnp.int32, (sw, sbs * A_DK), 0) >> log_c)
            == (lax.broadcasted_iota(jnp.int32, (sw, sbs * A_DK), 1) >> int(math.log2(A_DK))))

    def spread(cols):
        out = cols[:, sbs - 1:sbs]
        for hp in range(sbs - 2, -1, -1):
            out = jnp.where(blk_i == hp, cols[:, hp:hp + 1], out)
        return out

    def block_diag(m, mask):
        return jnp.where(mask, jnp.concatenate([m] * sbs, axis=0), jnp.zeros((), m.dtype))

    units = [(g, j) for g in range(A_HEADS // sbs) for j in range(n_chunks)]
    a_qk, t_inv, pw = {}, {}, {}
    for g, j in units:
        rows = slice(j * c, (j + 1) * c)
        hcols = slice(g * sbs * A_DK, (g + 1) * sbs * A_DK)
        qg = q_ref[rows, hcols].astype(BF16)
        kg = k_ref[rows, hcols].astype(BF16)
        qk_kk = lax.dot_general(jnp.concatenate([qg, kg], axis=0), block_diag(kg, bd_k), _NT,
                                preferred_element_type=F32)
        gcol = spread(gcum_c[rows, g * sbs:(g + 1) * sbs])
        grow = gcum_rot[(sbs - 1 - j) % n_chunks][g * sbs + sbs - 1:g * sbs + sbs, :]
        for hp in range(sbs - 2, -1, -1):
            grow = jnp.where(blk_i[0:1] == hp,
                             gcum_rot[(hp - j) % n_chunks][g * sbs + hp:g * sbs + hp + 1, :], grow)
        decay = jnp.where(incl, jnp.exp(jnp.minimum(gcol - grow, 0.0)), 0.0)
        a_qk[g, j] = qk_kk[0:c] * decay
        low = jnp.where(strict, qk_kk[c:2 * c] * decay * spread(beta_c[rows, g * sbs:(g + 1) * sbs]), 0.0)
        t_inv[g, j] = eye - low
        pw[g, j] = (-low).astype(BF16)

    for rnd in range(log_c):
        if 1 <= rnd <= A_HEADS // 2:
            gate_proj(A_HEADS // 2 + rnd - 1)
        for u in units:
            bd = block_diag(pw[u], bd_p)
            if rnd == 0:
                pw[u] = jnp.dot(pw[u], bd, preferred_element_type=F32).astype(BF16)
            elif rnd == log_c - 1:
                t_inv[u] = t_inv[u] + jnp.dot(t_inv[u].astype(BF16), bd, preferred_element_type=F32)
            else:
                res = jnp.dot(jnp.concatenate([pw[u], t_inv[u].astype(BF16)], axis=0), bd,
                              preferred_element_type=F32)
                pw[u] = res[0:c].astype(BF16)
                t_inv[u] = t_inv[u] + res[c:2 * c]

    st = {h: state_ref[h] for h in range(A_HEADS)}
    o_parts = {h: [] for h in range(A_HEADS)}
    for j in range(n_chunks):
        rows = slice(j * c, (j + 1) * c)
        uw, a_h, q_dec, k_dec = {}, {}, {}, {}
        for h in range(A_HEADS):
            g, hp = divmod(h, sbs)
            qh = q_ref[rows, h * A_DK:(h + 1) * A_DK]
            kh = k_ref[rows, h * A_DK:(h + 1) * A_DK]
            vh = v_ref[rows, h * A_DV:(h + 1) * A_DV]
            gc = gcum_c[rows, h:h + 1]
            bc = beta_c[rows, h:h + 1]
            eg = jnp.exp(gc)
            rhs = jnp.concatenate([(vh * bc).astype(BF16), (kh * (bc * eg)).astype(BF16)], axis=1)
            q_dec[h] = (qh * eg).astype(BF16)
            k_dec[h] = (kh * jnp.exp(gtot_c[rows, h:h + 1] - gc)).astype(BF16)
            t_hc = t_inv[g, j][:, hp * c:(hp + 1) * c].astype(BF16)
            uw[h] = jnp.dot(t_hc, rhs, preferred_element_type=F32)
            a_h[h] = a_qk[g, j][:, hp * c:(hp + 1) * c].astype(BF16)
        wq = {}
        for h in range(A_HEADS):
            w = uw[h][:, A_DV:].astype(BF16)
            wq[h] = jnp.dot(jnp.concatenate([w, q_dec[h]], axis=0), st[h].astype(BF16),
                            preferred_element_type=F32)
        for h in range(A_HEADS):
            v_new = (uw[h][:, 0:A_DV] - wq[h][0:c]).astype(BF16)
            o_parts[h].append(wq[h][c:2 * c] + jnp.dot(a_h[h], v_new, preferred_element_type=F32))
            g_last = jnp.exp(gtot_c[j * c:j * c + 1, h:h + 1])
            st[h] = st[h] * g_last + lax.dot_general(k_dec[h], v_new, _TN,
                                                     preferred_element_type=F32)

    for h in range(A_HEADS):
        state_ref[h] = st[h]
        o = _rms_rows(jnp.concatenate(o_parts[h], axis=0), ogain_ref[0])
        og_ref[:, h * A_DV:(h + 1) * A_DV] = (o * gate_ref[:, h * A_DV:(h + 1) * A_DV]).astype(BF16)

    out_ref[0] = x + jnp.dot(og_ref[...], wout_ref[0], preferred_element_type=F32)


def _deltanet_layer(x, layer, norm, w_in, w_ab, conv_w, a_log, dt_bias, o_gain, w_out):
    b, t, d = x.shape
    tm = A_TILE
    n = norm.shape[0]
    in_specs = [
        pl.BlockSpec((1, tm, d), lambda bi, ti: (bi, ti, 0)),
        _layer_spec((1, d), layer),
        _layer_spec((d, A_IN_W), layer),
        _layer_spec((d, LANES), layer),
        _layer_spec((CONV_W, A_CONV_CH), layer),
        _layer_spec((1, A_HEADS), layer),
        _layer_spec((1, A_HEADS), layer),
        _layer_spec((A_HEADS, 1), layer),
        _layer_spec((A_HEADS, 1), layer),
        _layer_spec((1, A_DV), layer),
        _layer_spec((A_V_W, d), layer),
    ]
    scratch = [pltpu.VMEM((tm + CONV_PAD, PROJ_SLAB), F32)
               for _ in range(A_CONV_CH // PROJ_SLAB)] + [
        pltpu.VMEM((A_HEADS, A_DK, A_DV), F32),
        pltpu.VMEM((tm, A_QK_W), F32),
        pltpu.VMEM((tm, A_QK_W), F32),
        pltpu.VMEM((tm, A_V_W), F32),
        pltpu.VMEM((tm, A_V_W), F32),
        pltpu.VMEM((tm, A_V_W), BF16),
    ]
    return pl.pallas_call(
        _deltanet_kernel,
        out_shape=jax.ShapeDtypeStruct((b, t, d), F32),
        grid=(b, t // tm),
        in_specs=in_specs,
        out_specs=pl.BlockSpec((1, tm, d), lambda bi, ti: (bi, ti, 0)),
        scratch_shapes=scratch,
        compiler_params=pltpu.CompilerParams(
            dimension_semantics=("arbitrary", "arbitrary"), vmem_limit_bytes=VMEM_LIMIT),
        name="deltanet_layer",
    )(x, norm.reshape(n, 1, d), w_in, w_ab, conv_w,
      a_log.reshape(n, 1, A_HEADS), dt_bias.reshape(n, 1, A_HEADS),
      a_log.reshape(n, A_HEADS, 1), dt_bias.reshape(n, A_HEADS, 1),
      o_gain.reshape(n, 1, A_DV), w_out)


def _head_sum_matrix(width):
    r = lax.broadcasted_iota(jnp.int32, (width, width), 0) // B_HD
    c = lax.broadcasted_iota(jnp.int32, (width, width), 1) // B_HD
    return (r == c).astype(BF16)


def _head_rms(x, gain_tiled):
    width = x.shape[-1]
    summer = _head_sum_matrix(LANES)
    sq = (x * x).astype(BF16)
    parts = []
    for s in range(width // LANES):
        cols = slice(s * LANES, (s + 1) * LANES)
        parts.append(jnp.dot(sq[:, cols], summer, preferred_element_type=F32))
    ss = jnp.concatenate(parts, axis=-1) if len(parts) > 1 else parts[0]
    return x * lax.rsqrt(ss * (1.0 / B_HD) + EPS) * gain_tiled


def _shared_kv_kernel(x_ref, norm_ref, wkv_ref, kgain_ref, k_ref, v_ref):
    hb = _rms_rows(x_ref[0], norm_ref[...]).astype(BF16)
    kv = jnp.dot(hb, wkv_ref[...], preferred_element_type=F32)
    k_ref[0] = _head_rms(kv[:, :B_KV_W], kgain_ref[...]).astype(BF16)
    v_ref[0] = kv[:, B_KV_W:].astype(BF16)


def _shared_kv(x, kv_norm, w_kv, k_gain):
    b, t, d = x.shape
    tk = min(KV_TILE, t)
    out = jax.ShapeDtypeStruct((b, t, B_KV_W), BF16)
    spec_out = pl.BlockSpec((1, tk, B_KV_W), lambda bi, ti: (bi, ti, 0))
    return pl.pallas_call(
        _shared_kv_kernel,
        out_shape=(out, out),
        grid=(b, t // tk),
        in_specs=[pl.BlockSpec((1, tk, d), lambda bi, ti: (bi, ti, 0)),
                  _const_spec((1, d)), _const_spec((d, 2 * B_KV_W)), _const_spec((1, B_KV_W))],
        out_specs=(spec_out, spec_out),
        compiler_params=pltpu.CompilerParams(
            dimension_semantics=("arbitrary", "arbitrary"), vmem_limit_bytes=VMEM_LIMIT),
        name="shared_kv",
    )(x, kv_norm.reshape(1, d), w_kv.astype(BF16), jnp.tile(k_gain, B_KV_HEADS).reshape(1, B_KV_W))


def _bucket_ranges():
    dist = np.arange(WINDOW)
    max_exact = N_BUCKETS // 2
    large = max_exact + (np.log(np.maximum(dist, 1) / max_exact) / np.log(MAX_DIST / max_exact)
                         * (N_BUCKETS - max_exact)).astype(np.int64)
    large = np.minimum(large, N_BUCKETS - 1)
    bucket = np.where(dist < max_exact, dist, large)
    ranges = []
    for bkt in range(N_BUCKETS):
        idx = np.nonzero(bucket == bkt)[0]
        if idx.size:
            assert np.all(np.diff(idx) == 1)
            ranges.append((bkt, int(idx[0]), int(idx[-1])))
    return ranges


def _band_bias_kernel(rel_ref, out_ref):
    h0 = pl.program_id(0) * BIAS_HEADS_PER_STEP
    qi = lax.broadcasted_iota(jnp.int32, (BLOCK, BLOCK), 0)
    si = lax.broadcasted_iota(jnp.int32, (BLOCK, BLOCK), 1)
    dist = jnp.where(si > qi, qi - si + BLOCK, qi - si)
    ranges = _bucket_ranges()
    assert ranges[0][1] == 0 and ranges[-1][2] == WINDOW - 1
    in_bucket = [(bkt, (dist >= lo) & (dist <= hi)) for bkt, lo, hi in ranges]
    for i in range(BIAS_HEADS_PER_STEP):
        acc = jnp.zeros((BLOCK, BLOCK), F32)
        for bkt, hit in in_bucket:
            acc = jnp.where(hit, rel_ref[bkt, h0 + i] * LOG2E, acc)
        out_ref[i] = acc


def _band_bias(rel_bias):
    return pl.pallas_call(
        _band_bias_kernel,
        out_shape=jax.ShapeDtypeStruct((B_Q_HEADS, BLOCK, BLOCK), F32),
        grid=(B_Q_HEADS // BIAS_HEADS_PER_STEP,),
        in_specs=[pl.BlockSpec(memory_space=pltpu.SMEM)],
        out_specs=pl.BlockSpec((BIAS_HEADS_PER_STEP, BLOCK, BLOCK), lambda h: (h, 0, 0)),
        compiler_params=pltpu.CompilerParams(dimension_semantics=("arbitrary",)),
        name="band_bias",
    )(rel_bias)


def _swa_kernel(layer, sinks_ref, x_ref, norm_ref, win_ref, qgain_ref,
                kprev_ref, kcur_ref, vprev_ref, vcur_ref, bias_ref, wout_ref,
                out_ref,
                q_ref, ka_ref, kb_ref, va_ref, vb_ref, o_ref, gate_ref):
    tb = B_TILE
    first = pl.program_id(1) == 0
    x = x_ref[0]
    hb = _rms_rows(x, norm_ref[0]).astype(BF16)
    sinks = [sinks_ref[layer, hq] * LOG2E for hq in range(B_Q_HEADS)]
    group_w = SWA_HEAD_GROUP * B_HD

    def q_proj(s):
        cols = slice(s * group_w, (s + 1) * group_w)
        q = jnp.dot(hb, win_ref[0, :, cols], preferred_element_type=F32)
        q_ref[:, cols] = _head_rms(q, qgain_ref[0, :, cols] * (B_HD ** -0.5 * LOG2E)).astype(BF16)

    def gate_proj(p):
        wz = win_ref[0, :, B_W + p * GATE_SLAB:B_W + (p + 1) * GATE_SLAB]
        gate_ref[:, p * GATE_SLAB:(p + 1) * GATE_SLAB] = _silu(jnp.dot(hb, wz, preferred_element_type=F32))

    lane = lax.broadcasted_iota(jnp.int32, (BLOCK + tb, LANES), 1)
    low_half = lane < B_HD
    for t2 in range(B_KV_W // LANES):
        cols = slice(t2 * LANES, (t2 + 1) * LANES)
        for src_prev, src_cur, dst_a, dst_b in ((kprev_ref, kcur_ref, ka_ref, kb_ref),
                                                (vprev_ref, vcur_ref, va_ref, vb_ref)):
            full = jnp.concatenate([src_prev[0, :, cols], src_cur[0, :, cols]], axis=0)
            swapped = pltpu.roll(full, B_HD, 1)
            zero = jnp.zeros_like(full)
            dst_a[2 * t2] = jnp.where(low_half, full, zero).astype(BF16)
            dst_b[2 * t2] = jnp.where(low_half, zero, swapped).astype(BF16)
            dst_a[2 * t2 + 1] = jnp.where(low_half, swapped, zero).astype(BF16)
            dst_b[2 * t2 + 1] = jnp.where(low_half, zero, full).astype(BF16)

    lane_q = lax.broadcasted_iota(jnp.int32, (BLOCK, LANES), 1)
    from_prev = (lax.broadcasted_iota(jnp.int32, (BLOCK, BLOCK), 1)
                 > lax.broadcasted_iota(jnp.int32, (BLOCK, BLOCK), 0))
    pen = jnp.where(first & from_prev, NEG, 0.0)

    groups = [(qb, h0) for qb in range(tb // BLOCK) for h0 in range(0, B_Q_HEADS, SWA_HEAD_GROUP)]

    def qk_logits(qb, h0):
        rows = slice(qb * BLOCK, (qb + 1) * BLOCK)
        band = slice(qb * BLOCK, qb * BLOCK + 2 * BLOCK)
        j = h0 // B_GROUP
        kz = (ka_ref[j, band, :], kb_ref[j, band, :])
        out = {}
        for hq in range(h0, h0 + SWA_HEAD_GROUP):
            qpair = q_ref[rows, (hq // 2) * LANES:(hq // 2 + 1) * LANES]
            lg2 = lax.dot_general(qpair, kz[hq % 2], _NT, preferred_element_type=F32)
            lg = jnp.where(from_prev, lg2[:, 0:BLOCK], lg2[:, BLOCK:2 * BLOCK]) + bias_ref[hq]
            out[hq] = lg + pen if qb == 0 else lg
        return out

    def softmax_pv(qb, h0, logits):
        rows = slice(qb * BLOCK, (qb + 1) * BLOCK)
        band = slice(qb * BLOCK, qb * BLOCK + 2 * BLOCK)
        j = h0 // B_GROUP
        vz = (va_ref[j, band, :], vb_ref[j, band, :])
        heads = range(h0, h0 + SWA_HEAD_GROUP)
        m, pexp, inv = {}, {}, {}
        for hq in heads:
            m[hq] = jnp.maximum(jnp.max(logits[hq], axis=-1, keepdims=True), sinks[hq])
        for hq in heads:
            pe = jnp.exp2(logits[hq] - m[hq])
            denom = jnp.sum(pe, axis=-1, keepdims=True) + jnp.exp2(sinks[hq] - m[hq])
            inv[hq] = 1.0 / denom
            pexp[hq] = jnp.concatenate([jnp.where(from_prev, pe, 0.0).astype(BF16),
                                        jnp.where(from_prev, 0.0, pe).astype(BF16)], axis=1)
        for hq in heads[::2]:
            pv = (jnp.dot(pexp[hq], vz[0], preferred_element_type=F32)
                  + jnp.dot(pexp[hq + 1], vz[1], preferred_element_type=F32))
            p = hq // 2
            o_ref[rows, p * LANES:(p + 1) * LANES] = pv * jnp.where(lane_q < B_HD, inv[hq], inv[hq + 1])

    n_gate = B_W // GATE_SLAB
    groups_per_block = B_Q_HEADS // SWA_HEAD_GROUP
    q_proj(0)
    logits = qk_logits(*groups[0])
    for gi, grp in enumerate(groups):
        if gi + 1 < groups_per_block:
            q_proj(gi + 1)
        nxt = qk_logits(*groups[gi + 1]) if gi + 1 < len(groups) else None
        softmax_pv(*grp, logits)
        for p in range(gi * n_gate // groups_per_block, min(n_gate, (gi + 1) * n_gate // groups_per_block)):
            gate_proj(p)
        if (gi + 1) % groups_per_block == 0:
            rows = slice(grp[0] * BLOCK, (grp[0] + 1) * BLOCK)
            og = (o_ref[rows, :] * gate_ref[rows, :]).astype(BF16)
            out_ref[0, rows, :] = x_ref[0, rows, :] + jnp.dot(og, wout_ref[0], preferred_element_type=F32)
        logits = nxt


def _swa_layer(x, layer, norm, w_in, q_gain_tiled, sinks, w_out, k, v, bias):
    b, t, d = x.shape
    tb = B_TILE
    nblk = tb // BLOCK
    n = norm.shape[0]
    tile = lambda bi, ti: (bi, ti, 0)
    prev = lambda bi, ti: (bi, jnp.maximum(ti * nblk - 1, 0), 0)
    in_specs = [
        pl.BlockSpec(memory_space=pltpu.SMEM),
        pl.BlockSpec((1, tb, d), tile),
        _layer_spec((1, d), layer),
        _layer_spec((d, 2 * B_W), layer),
        _layer_spec((1, B_W), layer),
        pl.BlockSpec((1, BLOCK, B_KV_W), prev),
        pl.BlockSpec((1, tb, B_KV_W), tile),
        pl.BlockSpec((1, BLOCK, B_KV_W), prev),
        pl.BlockSpec((1, tb, B_KV_W), tile),
        _const_spec((B_Q_HEADS, BLOCK, BLOCK)),
        _layer_spec((B_W, d), layer),
    ]
    kv_scratch = pltpu.VMEM((B_KV_HEADS, BLOCK + tb, LANES), BF16)
    scratch = [pltpu.VMEM((tb, B_W), BF16), kv_scratch, kv_scratch, kv_scratch, kv_scratch,
               pltpu.VMEM((tb, B_W), F32), pltpu.VMEM((tb, B_W), F32)]
    return pl.pallas_call(
        functools.partial(_swa_kernel, layer),
        out_shape=jax.ShapeDtypeStruct((b, t, d), F32),
        grid=(b, t // tb),
        in_specs=in_specs,
        out_specs=pl.BlockSpec((1, tb, d), tile),
        scratch_shapes=scratch,
        compiler_params=pltpu.CompilerParams(
            dimension_semantics=("arbitrary", "arbitrary"), vmem_limit_bytes=VMEM_LIMIT),
        name="swa_layer",
    )(sinks, x, norm.reshape(n, 1, d), w_in, q_gain_tiled, k, k, v, v, bias, w_out)


def kernel(x, a_norm, a_w_in, a_conv, a_A_log, a_dt_bias, a_o_gain, a_w_out, kv_norm, w_kv, k_gain, rel_bias, b_norm, b_w_in, b_q_gain, b_sinks, b_w_out):
    n_a = a_w_in.shape[0]
    n_b = b_w_in.shape[0]
    a_w_in_b = a_w_in.astype(BF16)
    a_w_ab = jnp.pad(a_w_in[:, :, A_CONV_CH + A_V_W:],
                     ((0, 0), (0, 0), (0, LANES - 2 * A_HEADS))).astype(BF16)
    a_w_out_b = a_w_out.astype(BF16)
    b_w_in_b = b_w_in.astype(BF16)
    b_w_out_b = b_w_out.astype(BF16)
    q_gain_tiled = jnp.tile(b_q_gain, (1, B_Q_HEADS)).reshape(n_b, 1, B_W)
    for i in range(n_a):
        x = _deltanet_layer(x, i, a_norm, a_w_in_b, a_w_ab, a_conv, a_A_log, a_dt_bias,
                            a_o_gain, a_w_out_b)
    k, v = _shared_kv(x, kv_norm, w_kv, k_gain)
    bias = _band_bias(rel_bias)
    for j in range(n_b):
        x = _swa_layer(x, j, b_norm, b_w_in_b, q_gain_tiled, b_sinks, b_w_out_b, k, v, bias)
    return x
```
